```python
import math
import jax
import jax.numpy as jnp
from jax import lax
import numpy as np

D_MODEL = 1024
BATCH = 2
SEQ = 16384
DEPTH = 2

GRID_W = 64
CTX_LEN = 256
EPS = 1e-6

S5_WIDTH = 512
S5_GROUP = 16
S5_GROUPS = S5_WIDTH // S5_GROUP
S5_STATE = 64

SSD_WIDTH = 512
SSD_HEADDIM = 64
SSD_HEADS = SSD_WIDTH // SSD_HEADDIM
SSD_GROUPS = 2
SSD_STATE = 128
SSD_CONV = 3
SSD_CHUNK = 256
SSD_CONV_CH = SSD_WIDTH + 2 * SSD_GROUPS * SSD_STATE

ATT_HEADDIM = 64
ATT_HEADS = 8
ATT_KV_HEADS = 2
ATT_WIDTH = ATT_HEADS * ATT_HEADDIM
ATT_BLOCK = 128
ROPE_THETA = 10000.0

N_BRANCH = 3
IN_SPLITS = (S5_WIDTH, S5_WIDTH, SSD_CONV_CH, SSD_WIDTH, 2 * SSD_HEADS, ATT_WIDTH, ATT_KV_HEADS * ATT_HEADDIM, ATT_KV_HEADS * ATT_HEADDIM, ATT_WIDTH, N_BRANCH * D_MODEL)
IN_WIDTH = 2 * S5_WIDTH + SSD_CONV_CH + SSD_WIDTH + 2 * SSD_HEADS + 2 * ATT_WIDTH + 2 * ATT_KV_HEADS * ATT_HEADDIM + N_BRANCH * D_MODEL

kernel_name = 'hybrid_s5_ssd_gqa_prefix_dit_block'


def rmsnorm(x, w):
    x32 = x.astype(jnp.float32)
    y = x32 * lax.rsqrt(jnp.mean(x32 * x32, axis=-1, keepdims=True) + EPS)
    return (y * w.astype(jnp.float32)).astype(x.dtype)


def split_cols(z):
    idx = np.cumsum(IN_SPLITS)[:-1].tolist()
    return jnp.split(z, idx, axis=-1)


def axial_rope_tables(rows):
    half = ATT_HEADDIM // 4
    inv_freq = ROPE_THETA ** (-jnp.arange(half, dtype=jnp.float32) / half)
    row = jnp.repeat(jnp.arange(rows, dtype=jnp.float32), GRID_W)
    col = jnp.tile(jnp.arange(GRID_W, dtype=jnp.float32), rows)
    ang = jnp.concatenate([row[:, None] * inv_freq, col[:, None] * inv_freq], axis=-1)
    return jnp.cos(ang), jnp.sin(ang)


def apply_rope(t, cos, sin):
    t32 = t.astype(jnp.float32)
    t1, t2 = jnp.split(t32, 2, axis=-1)
    c = cos[None, :, None, :]
    s = sin[None, :, None, :]
    return jnp.concatenate([t1 * c - t2 * s, t1 * s + t2 * c], axis=-1).astype(t.dtype)


def depthwise_conv(x, w, b):
    k = w.shape[0]
    y = lax.conv_general_dilated(x, w[:, None, :].astype(x.dtype), window_strides=(1,), padding=[((k - 1) // 2, k // 2)], dimension_numbers=('NWC', 'WIO', 'NWC'), feature_group_count=x.shape[-1])
    return y + b.astype(x.dtype)


def s5_discretise(lam_re, lam_im, log_dt, b_re, b_im):
    lam = lax.complex(lam_re.astype(jnp.float32), lam_im.astype(jnp.float32))
    dt = jnp.exp(log_dt.astype(jnp.float32))[:, None]
    a_bar = jnp.exp(lam * dt)
    b = lax.complex(b_re.astype(jnp.float32), b_im.astype(jnp.float32))
    b_bar = ((a_bar - 1.0) / lam)[..., None] * b
    return a_bar, b_bar


def _lin_combine(left, right):
    a_l, b_l = left
    a_r, b_r = right
    return a_l * a_r, a_r * b_l + b_r


def s5_scan(bu, a_bar, h0, reverse):
    n = bu.shape[1]
    if h0 is not None:
        edge = n - 1 if reverse else 0
        bu = bu.at[:, edge].add(a_bar * h0)
    a = jnp.broadcast_to(a_bar, (1, n) + a_bar.shape)
    _, h = lax.associative_scan(_lin_combine, (a, bu), reverse=reverse, axis=1)
    return h


def s5_output(y, u, gate, d, w_glu):
    y = jax.nn.gelu(y.reshape(u.shape).astype(u.dtype) + d * u)
    y = y * jax.nn.sigmoid(y @ w_glu)
    return y * jax.nn.silu(gate)


def s5_branch(u_l, u_c, gate_l, gate_c, lam_re, lam_im, log_dt, b_re, b_im, c_re, c_im, d, w_glu, with_ctx):
    def grouped(u):
        return u.astype(jnp.float32).reshape(u.shape[:2] + (S5_GROUPS, S5_GROUP)).astype(jnp.complex64)
    ul, uc = grouped(u_l), grouped(u_c)
    ys_l, ys_c = [], []
    for k, rev in ((0, False), (1, True)):
        a_bar, b_bar = s5_discretise(lam_re[k], lam_im[k], log_dt[k], b_re[k], b_im[k])
        cm = lax.complex(c_re[k].astype(jnp.float32), c_im[k].astype(jnp.float32))
        h_c = s5_scan(jnp.einsum('bsgh,gph->bsgp', uc, b_bar), a_bar, None, rev)
        h0 = h_c[:, 0] if rev else h_c[:, -1]
        h_l = s5_scan(jnp.einsum('bsgh,gph->bsgp', ul, b_bar), a_bar, h0, rev)
        ys_l.append(jnp.real(jnp.einsum('ghp,bsgp->bsgh', cm, h_l)))
        if with_ctx:
            ys_c.append(jnp.real(jnp.einsum('ghp,bsgp->bsgh', cm, h_c)))
    out_l = s5_output(ys_l[0] + ys_l[1], u_l, gate_l, d, w_glu)
    out_c = s5_output(ys_c[0] + ys_c[1], u_c, gate_c, d, w_glu) if with_ctx else None
    return out_l, out_c


def ssd_chunked(x, dt, a, bmat, cmat, h0):
    bsz, n, nh, hp = x.shape
    g = bmat.shape[2]
    r = nh // g
    ds = bmat.shape[-1]
    lc = math.gcd(n, SSD_CHUNK)
    nc = n // lc
    x = x.reshape(bsz, nc, lc, g, r, hp)
    dt = dt.reshape(bsz, nc, lc, g, r)
    bmat = bmat.reshape(bsz, nc, lc, g, ds)
    cmat = cmat.reshape(bsz, nc, lc, g, ds)
    a_cum = jnp.cumsum(dt * a.reshape(g, r), axis=2)
    causal = jnp.tril(jnp.ones((lc, lc), dtype=bool))[:, :, None, None]
    seg = a_cum[:, :, :, None] - a_cum[:, :, None, :]
    decay = jnp.exp(jnp.where(causal, seg, -jnp.inf))
    cb = jnp.einsum('bclgn,bcsgn->bclsg', cmat, bmat)
    w = cb[..., None] * decay * dt[:, :, None]
    y_diag = jnp.einsum('bclsgr,bcsgrp->bclgrp', w, x)
    decay_end = jnp.exp(a_cum[:, :, -1:] - a_cum)
    states = jnp.einsum('bclgn,bclgr,bclgrp->bcgrpn', bmat, decay_end * dt, x)
    chunk_decay = jnp.exp(a_cum[:, :, -1])

    def step(h, inp):
        dec, st = inp
        return dec[..., None, None] * h + st, h

    h_last, h_prev = lax.scan(step, h0, (jnp.moveaxis(chunk_decay, 1, 0), jnp.moveaxis(states, 1, 0)))
    h_prev = jnp.moveaxis(h_prev, 0, 1)
    y_off = jnp.einsum('bclgn,bcgrpn,bclgr->bclgrp', cmat, h_prev, jnp.exp(a_cum))
    return (y_diag + y_off).reshape(bsz, n, nh, hp), h_last


def ssd_direction(xs, dts, bm, cm, a, h0, reverse):
    f = (lambda t: jnp.flip(t, axis=1)) if reverse else (lambda t: t)
    y, h = ssd_chunked(f(xs), f(dts), a, f(bm), f(cm), h0)
    return f(y), h


def ssd_branch(xbc_l, xbc_c, z_l, z_c, dt_l, dt_c, conv_w, conv_b, dt_bias, a_log, d, norm_w, with_ctx):
    def prep(xbc):
        xbc = jax.nn.silu(depthwise_conv(xbc, conv_w, conv_b)).astype(jnp.float32)
        bsz, n = xbc.shape[:2]
        xs, bm, cm = jnp.split(xbc, [SSD_WIDTH, SSD_WIDTH + SSD_GROUPS * SSD_STATE], axis=-1)
        return (xs.reshape(bsz, n, SSD_HEADS, SSD_HEADDIM), bm.reshape(bsz, n, SSD_GROUPS, SSD_STATE), cm.reshape(bsz, n, SSD_GROUPS, SSD_STATE))
    xl, bl, cl = prep(xbc_l)
    xc, bc, cc = prep(xbc_c)
    d32 = d.astype(jnp.float32)[:, None]
    y_l = d32 * xl
    y_c = d32 * xc
    h_zero = jnp.zeros((xc.shape[0], SSD_GROUPS, SSD_HEADS // SSD_GROUPS, SSD_HEADDIM, SSD_STATE), jnp.float32)
    for k, rev in ((0, False), (1, True)):
        a = -jnp.exp(a_log[k].astype(jnp.float32))
        bias = dt_bias[k].astype(jnp.float32)
        dtl = jax.nn.softplus(dt_l[..., k * SSD_HEADS:(k + 1) * SSD_HEADS].astype(jnp.float32) + bias)
        dtc = jax.nn.softplus(dt_c[..., k * SSD_HEADS:(k + 1) * SSD_HEADS].astype(jnp.float32) + bias)
        yc_k, h_ctx = ssd_direction(xc, dtc, bc, cc, a, h_zero, rev)
        yl_k, _ = ssd_direction(xl, dtl, bl, cl, a, h_ctx, rev)
        y_l = y_l + yl_k
        y_c = y_c + yc_k
    bsz, n = z_l.shape[:2]
    out_l = rmsnorm(y_l.reshape(bsz, n, SSD_WIDTH).astype(z_l.dtype) * jax.nn.silu(z_l), norm_w)
    out_c = None
    if with_ctx:
        out_c = rmsnorm(y_c.reshape(z_c.shape).astype(z_c.dtype) * jax.nn.silu(z_c), norm_w)
    return out_l, out_c


def attention_branch(q_l, k_l, v_l, q_c, k_c, v_c, gate_l, gate_c, q_norm, k_norm, cos, sin, with_ctx):
    bsz, n = q_l.shape[:2]
    n_ctx = q_c.shape[1]
    grp = ATT_HEADS // ATT_KV_HEADS
    scale = ATT_HEADDIM ** -0.5

    def heads(t, h):
        return t.reshape(t.shape[:2] + (h, ATT_HEADDIM))

    q_l = apply_rope(rmsnorm(heads(q_l, ATT_HEADS), q_norm), cos, sin)
    k_l = apply_rope(rmsnorm(heads(k_l, ATT_KV_HEADS), k_norm), cos, sin)
    q_c = rmsnorm(heads(q_c, ATT_HEADS), q_norm)
    k_c = rmsnorm(heads(k_c, ATT_KV_HEADS), k_norm)
    v_l = heads(v_l, ATT_KV_HEADS)
    v_c = heads(v_c, ATT_KV_HEADS)
    k_all = jnp.concatenate([k_l, k_c], axis=1)
    v_all = jnp.concatenate([v_l, v_c], axis=1)

    def attend(qb, kk, vv):
        s = jnp.einsum('bqhgd,bshd->bhgqs', qb, kk).astype(jnp.float32) * scale
        p = jax.nn.softmax(s, axis=-1).astype(vv.dtype)
        return jnp.einsum('bhgqs,bshd->bqhgd', p, vv)

    nb = n // ATT_BLOCK
    qb = q_l.reshape(bsz, nb, ATT_BLOCK, ATT_KV_HEADS, grp, ATT_HEADDIM).transpose(1, 0, 2, 3, 4, 5)
    o = lax.map(lambda blk: attend(blk, k_all, v_all), qb)
    out_l = o.transpose(1, 0, 2, 3, 4, 5).reshape(bsz, n, ATT_WIDTH) * jax.nn.silu(gate_l)
    out_c = None
    if with_ctx:
        oc = attend(q_c.reshape(bsz, n_ctx, ATT_KV_HEADS, grp, ATT_HEADDIM), k_c, v_c)
        out_c = oc.reshape(bsz, n_ctx, ATT_WIDTH) * jax.nn.silu(gate_c)
    return out_l, out_c


def merge_branches(ya, yb, yc, gate_logits, w_a, w_b, w_c, w_o):
    ga, gb, gc = jnp.split(jax.nn.sigmoid(gate_logits), N_BRANCH, axis=-1)
    return (ga * (ya @ w_a) + gb * (yb @ w_b) + gc * (yc @ w_c)) @ w_o


def hybrid_layer(x, xc, c, c_ctx, cos, sin, mod_w, mod_b, pre_norm, post_norm, w_in, s5_lam_re, s5_lam_im, s5_log_dt, s5_b_re, s5_b_im, s5_c_re, s5_c_im, s5_d, s5_glu, ssd_conv_w, ssd_conv_b, ssd_dt_bias, ssd_a_log, ssd_d, ssd_norm, att_q_norm, att_k_norm, w_branch_a, w_branch_b, w_branch_c, w_out, with_ctx):
    mod_l = jax.nn.silu(c) @ mod_w + mod_b
    mod_c = jax.nn.silu(c_ctx) @ mod_w + mod_b
    sh_l, sc_l, g_l = jnp.split(mod_l[:, None, :], 3, axis=-1)
    sh_c, sc_c, g_c = jnp.split(mod_c, 3, axis=-1)
    h_l = rmsnorm(x, pre_norm) * (1 + sc_l) + sh_l
    h_c = rmsnorm(xc, pre_norm) * (1 + sc_c) + sh_c
    u_l, ug_l, xbc_l, z_l, dt_l, q_l, k_l, v_l, ag_l, mg_l = split_cols(h_l @ w_in)
    u_c, ug_c, xbc_c, z_c, dt_c, q_c, k_c, v_c, ag_c, mg_c = split_cols(h_c @ w_in)

    ya_l, ya_c = s5_branch(u_l, u_c, ug_l, ug_c, s5_lam_re, s5_lam_im, s5_log_dt, s5_b_re, s5_b_im, s5_c_re, s5_c_im, s5_d, s5_glu, with_ctx)
    yb_l, yb_c = ssd_branch(xbc_l, xbc_c, z_l, z_c, dt_l, dt_c, ssd_conv_w, ssd_conv_b, ssd_dt_bias, ssd_a_log, ssd_d, ssd_norm, with_ctx)
    yc_l, yc_c = attention_branch(q_l, k_l, v_l, q_c, k_c, v_c, ag_l, ag_c, att_q_norm, att_k_norm, cos, sin, with_ctx)

    out_l = merge_branches(ya_l, yb_l, yc_l, mg_l, w_branch_a, w_branch_b, w_branch_c, w_out)
    x = x + g_l * rmsnorm(out_l, post_norm)
    if not with_ctx:
        return x, None
    out_c = merge_branches(ya_c, yb_c, yc_c, mg_c, w_branch_a, w_branch_b, w_branch_c, w_out)
    xc = xc + g_c * rmsnorm(out_c, post_norm)
    return x, xc


def setup_inputs(seed: int = 0) -> dict:
    key = jax.random.key(seed)
    ks = iter(jax.random.split(key, 40))
    f32 = jnp.float32

    def nrm(shape, scale):
        return scale * jax.random.normal(next(ks), shape, f32)

    L, D = DEPTH, D_MODEL
    G, P, H = S5_GROUPS, S5_STATE, S5_GROUP
    x = nrm((BATCH, SEQ, D), 1.0)
    c = nrm((BATCH, D), 1.0)
    ctx = nrm((BATCH, CTX_LEN, D), 1.0)
    c_ctx = nrm((D,), 1.0)
    mod_w = nrm((L, D, 3 * D), 0.5 * D ** -0.5)
    mod_b = nrm((L, 3 * D), 0.02)
    pre_norm = 1.0 + nrm((L, D), 0.05)
    post_norm = 1.0 + nrm((L, D), 0.05)
    w_in = nrm((L, D, IN_WIDTH), D ** -0.5)
    s5_lam_re = -0.5 * jnp.exp(nrm((L, 2, G, P), 0.05))
    s5_lam_im = jnp.pi * jnp.arange(P, dtype=f32) + nrm((L, 2, G, P), 0.01)
    s5_log_dt = jax.random.uniform(next(ks), (L, 2, G), f32, math.log(1e-3), math.log(1e-1))
    s5_b_re = nrm((L, 2, G, P, H), (2 * H) ** -0.5)
    s5_b_im = nrm((L, 2, G, P, H), (2 * H) ** -0.5)
    s5_c_re = nrm((L, 2, G, H, P), (2 * P) ** -0.5)
    s5_c_im = nrm((L, 2, G, H, P), (2 * P) ** -0.5)
    s5_d = nrm((L, S5_WIDTH), 1.0)
    s5_glu = nrm((L, S5_WIDTH, S5_WIDTH), S5_WIDTH ** -0.5)
    ssd_conv_w = nrm((L, SSD_CONV, SSD_CONV_CH), SSD_CONV ** -0.5)
    ssd_conv_b = nrm((L, SSD_CONV_CH), 0.02)
    dt0 = jnp.exp(jax.random.uniform(next(ks), (L, 2, SSD_HEADS), f32, math.log(1e-3), math.log(1e-1)))
    ssd_dt_bias = dt0 + jnp.log(-jnp.expm1(-dt0))
    ssd_a_log = jnp.log(jax.random.uniform(next(ks), (L, 2, SSD_HEADS), f32, 1.0, 16.0))
    ssd_d = 1.0 + nrm((L, SSD_HEADS), 0.05)
    ssd_norm = 1.0 + nrm((L, SSD_WIDTH), 0.05)
    att_q_norm = 1.0 + nrm((L, ATT_HEADDIM), 0.05)
    att_k_norm = 1.0 + nrm((L, ATT_HEADDIM), 0.05)
    w_branch_a = nrm((L, S5_WIDTH, D), S5_WIDTH ** -0.5)
    w_branch_b = nrm((L, SSD_WIDTH, D), SSD_WIDTH ** -0.5)
    w_branch_c = nrm((L, ATT_WIDTH, D), ATT_WIDTH ** -0.5)
    w_out = nrm((L, D, D), D ** -0.5)
    return {'x': x, 'c': c, 'ctx': ctx, 'c_ctx': c_ctx, 'mod_w': mod_w, 'mod_b': mod_b, 'pre_norm': pre_norm, 'post_norm': post_norm, 'w_in': w_in, 's5_lam_re': s5_lam_re, 's5_lam_im': s5_lam_im, 's5_log_dt': s5_log_dt, 's5_b_re': s5_b_re, 's5_b_im': s5_b_im, 's5_c_re': s5_c_re, 's5_c_im': s5_c_im, 's5_d': s5_d, 's5_glu': s5_glu, 'ssd_conv_w': ssd_conv_w, 'ssd_conv_b': ssd_conv_b, 'ssd_dt_bias': ssd_dt_bias, 'ssd_a_log': ssd_a_log, 'ssd_d': ssd_d, 'ssd_norm': ssd_norm, 'att_q_norm': att_q_norm, 'att_k_norm': att_k_norm, 'w_branch_a': w_branch_a, 'w_branch_b': w_branch_b, 'w_branch_c': w_branch_c, 'w_out': w_out}


def reference(x, c, ctx, c_ctx, mod_w, mod_b, pre_norm, post_norm, w_in, s5_lam_re, s5_lam_im, s5_log_dt, s5_b_re, s5_b_im, s5_c_re, s5_c_im, s5_d, s5_glu, ssd_conv_w, ssd_conv_b, ssd_dt_bias, ssd_a_log, ssd_d, ssd_norm, att_q_norm, att_k_norm, w_branch_a, w_branch_b, w_branch_c, w_out):
    rows = x.shape[1] // GRID_W
    cos, sin = axial_rope_tables(rows)
    xc = ctx
    for l in range(DEPTH):
        with_ctx = l < DEPTH - 1
        x, xc = hybrid_layer(x, xc, c, c_ctx, cos, sin, mod_w[l], mod_b[l], pre_norm[l], post_norm[l], w_in[l], s5_lam_re[l], s5_lam_im[l], s5_log_dt[l], s5_b_re[l], s5_b_im[l], s5_c_re[l], s5_c_im[l], s5_d[l], s5_glu[l], ssd_conv_w[l], ssd_conv_b[l], ssd_dt_bias[l], ssd_a_log[l], ssd_d[l], ssd_norm[l], att_q_norm[l], att_k_norm[l], w_branch_a[l], w_branch_b[l], w_branch_c[l], w_out[l], with_ctx)
    return x
```

```python
import functools

import jax
import jax.numpy as jnp
from jax import lax
from jax.experimental import pallas as pl
from jax.experimental.pallas import tpu as pltpu

F32 = jnp.float32
BF16 = jnp.bfloat16

D = 1024
EPS = 1e-6
GRID_W = 64
S5_G, S5_H, S5_P = 32, 16, 64
S5_W = S5_G * S5_H
S5_SUB = 16
SSD_W, SSD_HD, SSD_NH, SSD_NG, SSD_N = 512, 64, 8, 2, 128
CHUNK = 256
ATT_HD, ATT_NH, ATT_NKV = 64, 8, 2
ATT_GRP = ATT_NH // ATT_NKV
ROPE_THETA = 10000.0
VT_ROWS = 80

C_MG, C_U, C_UG, C_XBC, C_Z, C_Q, C_AG, C_K, C_V, C_DT = 0, 3072, 3584, 4096, 5120, 5632, 6144, 6656, 6784, 6912
ZW = 7040
PROJ_TN = 640
VMEM_LIMIT = 56 * 1024 * 1024


def _cparams(sem):
    return pltpu.CompilerParams(dimension_semantics=sem, vmem_limit_bytes=VMEM_LIMIT)


def _silu(x):
    return x * jax.nn.sigmoid(x)


def _softplus(x):
    return jnp.maximum(x, 0.0) + jnp.log1p(jnp.exp(-jnp.abs(x)))


def _mod_kernel(c_ref, w_ref, b_ref, o_ref):
    c = c_ref[...]
    o_ref[0] = jnp.dot(_silu(c), w_ref[0], preferred_element_type=F32,
                       precision=lax.Precision.HIGHEST) + b_ref[0]


def _modulation(c8, mod_w, mod_b):
    nl = mod_w.shape[0]
    return pl.pallas_call(
        _mod_kernel,
        grid=(nl, 3),
        in_specs=[pl.BlockSpec((8, D), lambda l, j: (0, 0)),
                  pl.BlockSpec((1, D, D), lambda l, j: (l, 0, j)),
                  pl.BlockSpec((1, 1, D), lambda l, j: (l, 0, j))],
        out_specs=pl.BlockSpec((1, 8, D), lambda l, j: (l, 0, j)),
        out_shape=jax.ShapeDtypeStruct((nl, 8, 3 * D), F32),
        compiler_params=_cparams(("parallel", "parallel")),
        name="modulation",
    )(c8, mod_w, mod_b.reshape(nl, 1, 3 * D))


def _proj_kernel(x_ref, mod_ref, pn_ref, w_ref, o_ref, h_ref, *, n_lat, tm):
    i = pl.program_id(1)

    @pl.when(pl.program_id(2) == 0)
    def _():
        x = x_ref[0]
        y = x * lax.rsqrt(jnp.mean(x * x, axis=-1, keepdims=True) + EPS) * pn_ref[...]
        m = mod_ref[0]
        h_l = y * (1.0 + m[1:2]) + m[0:1]
        h_c = y * (1.0 + m[4:5]) + m[3:4]
        row = i * tm + lax.broadcasted_iota(jnp.int32, (tm, 1), 0)
        h_ref[...] = jnp.where(row >= n_lat, h_c, h_l).astype(BF16)

    o_ref[0] = jnp.dot(h_ref[...], w_ref[...], preferred_element_type=F32).astype(o_ref.dtype)


def _in_proj(x_all, modv, pre_norm, w, n_lat, tm):
    b, nt, _ = x_all.shape
    return pl.pallas_call(
        functools.partial(_proj_kernel, n_lat=n_lat, tm=tm),
        grid=(b, nt // tm, ZW // PROJ_TN),
        in_specs=[pl.BlockSpec((1, tm, D), lambda bi, i, j: (bi, i, 0)),
                  pl.BlockSpec((1, 8, D), lambda bi, i, j: (bi, 0, 0)),
                  pl.BlockSpec((1, D), lambda bi, i, j: (0, 0)),
                  pl.BlockSpec((D, PROJ_TN), lambda bi, i, j: (0, j))],
        out_specs=pl.BlockSpec((1, tm, PROJ_TN), lambda bi, i, j: (bi, i, j)),
        out_shape=jax.ShapeDtypeStruct((b, nt, ZW), BF16),
        scratch_shapes=[pltpu.VMEM((tm, D), BF16)],
        compiler_params=_cparams(("parallel", "parallel", "arbitrary")),
        name="in_proj",
    )(x_all, modv, pre_norm.reshape(1, D), w)


def _pack_w_in(w):
    u, ug, xbc, z, dt, q, k, v, ag, mg = jnp.split(
        w, [512, 1024, 2048, 2560, 2576, 3088, 3216, 3344, 3856], axis=-1)
    dt = jnp.pad(dt, ((0, 0), (0, 128 - dt.shape[-1])))
    return jnp.concatenate([mg, u, ug, xbc, z, q, ag, k, v, dt], axis=-1).astype(BF16)


def _s5_tables(lam_re, lam_im, log_dt, b_re, b_im, c_re, c_im):
    lam = lax.complex(lam_re.astype(F32), lam_im.astype(F32))
    ladt = lam * jnp.exp(log_dt.astype(F32))[..., None]
    b_bar = ((jnp.exp(ladt) - 1.0) / lam)[..., None] * lax.complex(b_re.astype(F32), b_im.astype(F32))
    cm = lax.complex(c_re.astype(F32), c_im.astype(F32))
    n = jnp.arange(S5_SUB, dtype=F32)

    def powers(e):
        return jnp.exp(e[:, None, None, None] * ladt[None])

    kd = jnp.real(jnp.einsum('kghp,dkgp,kgpj->dkghj', cm, powers(n), b_bar))
    s_idx = jnp.arange(S5_SUB)[:, None]
    t_idx = jnp.arange(S5_SUB)[None, :]
    toe = kd[jnp.clip(t_idx - s_idx, 0, S5_SUB - 1)]
    toe = jnp.where((t_idx >= s_idx)[:, :, None, None, None, None], toe, 0.0)
    toe = toe.transpose(2, 3, 0, 5, 1, 4).reshape(2, S5_G, 256, 256)
    wst = jnp.einsum('skgp,kgpj->kgsjp', powers(S5_SUB - 1.0 - n), b_bar).reshape(2, S5_G, 256, S5_P)
    wi = jnp.einsum('kghp,tkgp->kgpth', cm, powers(n + 1.0)).reshape(2, S5_G, S5_P, 256)
    pw = powers(S5_SUB * n).transpose(1, 2, 0, 3)
    a256 = jnp.exp(float(CHUNK) * ladt)[:, :, None, :]
    a256 = jnp.concatenate([jnp.real(a256), jnp.imag(a256), jnp.zeros((2, S5_G, 6, S5_P), F32)], axis=2)
    return dict(toe=toe.astype(BF16),
                wst_re=jnp.real(wst).astype(BF16), wst_im=jnp.imag(wst).astype(BF16),
                wi_re=jnp.real(wi).astype(BF16), wi_im=(-jnp.imag(wi)).astype(BF16),
                pw_re=jnp.real(pw), pw_im=jnp.imag(pw), a256=a256)


def _s5_kernel(u_ref, toe_ref, wre_ref, wim_ref, wire_ref, wiim_ref, pwre_ref, pwim_ref, a256_ref, y_ref,
               hre_ref, him_ref, ere_ref, eim_ref, gre_ref, gim_ref, *, rb, nblk, nb):
    u = u_ref[0, 0]
    s_re = jnp.dot(u, wre_ref[0, 0], preferred_element_type=F32)
    s_im = jnp.dot(u, wim_ref[0, 0], preferred_element_type=F32)
    a_re = pwre_ref[0, 0, 1:2, :]
    a_im = pwim_ref[0, 0, 1:2, :]
    p_re = jnp.zeros((rb, S5_P), F32)
    p_im = jnp.zeros((rb, S5_P), F32)
    for j in range(S5_SUB):
        hre_ref[j * rb:(j + 1) * rb, :] = p_re
        him_ref[j * rb:(j + 1) * rb, :] = p_im
        sr = s_re[j * rb:(j + 1) * rb]
        si = s_im[j * rb:(j + 1) * rb]
        p_re, p_im = a_re * p_re - a_im * p_im + sr, a_re * p_im + a_im * p_re + si
    ere_ref[...] = p_re
    eim_ref[...] = p_im
    gre_ref[...] = jnp.zeros_like(gre_ref)
    gim_ref[...] = jnp.zeros_like(gim_ref)
    b_re = a256_ref[0, 0, 0:1, :]
    b_im = a256_ref[0, 0, 1:2, :]

    def block_step(t, carry):
        for b in range(nb):
            r = b * nblk + t
            g_re = gre_ref[pl.ds(r, 1), :]
            g_im = gim_ref[pl.ds(r, 1), :]
            gre_ref[pl.ds(r + 1, 1), :] = b_re * g_re - b_im * g_im + ere_ref[pl.ds(r, 1), :]
            gim_ref[pl.ds(r + 1, 1), :] = b_re * g_im + b_im * g_re + eim_ref[pl.ds(r, 1), :]
        return carry

    lax.fori_loop(0, nblk - 1, block_step, 0)
    g_re = gre_ref[...]
    g_im = gim_ref[...]
    for j in range(S5_SUB):
        q_re = pwre_ref[0, 0, j:j + 1, :]
        q_im = pwim_ref[0, 0, j:j + 1, :]
        hre_ref[j * rb:(j + 1) * rb, :] += q_re * g_re - q_im * g_im
        him_ref[j * rb:(j + 1) * rb, :] += q_re * g_im + q_im * g_re
    y = jnp.dot(u, toe_ref[0, 0], preferred_element_type=F32)
    y += jnp.dot(hre_ref[...].astype(BF16), wire_ref[0, 0], preferred_element_type=F32)
    y += jnp.dot(him_ref[...].astype(BF16), wiim_ref[0, 0], preferred_element_type=F32)
    y_ref[0, 0] = y.astype(y_ref.dtype)


def _s5_scan(u, tabs, n_lat):
    b, nt, _ = u.shape
    nblk = nt // CHUNK
    rb = -(-(b * nblk) // 8) * 8
    ul, uc = u[:, :n_lat], u[:, n_lat:]
    seq = jnp.stack([jnp.concatenate([uc, ul], axis=1),
                     jnp.concatenate([uc[:, ::-1], ul[:, ::-1]], axis=1)])
    t = seq.reshape(2, b, nblk, S5_SUB, S5_SUB, S5_G, S5_H).transpose(0, 5, 3, 1, 2, 4, 6)
    t = t.reshape(2, S5_G, S5_SUB, b * nblk, 256)
    t = jnp.pad(t, ((0, 0), (0, 0), (0, 0), (0, rb - b * nblk), (0, 0))).reshape(2, S5_G, S5_SUB * rb, 256)
    r = S5_SUB * rb

    def spec(shape):
        return pl.BlockSpec((1, 1) + shape, lambda d, g: (d, g, 0, 0))

    y = pl.pallas_call(
        functools.partial(_s5_kernel, rb=rb, nblk=nblk, nb=b),
        grid=(2, S5_G),
        in_specs=[spec((r, 256)), spec((256, 256)), spec((256, S5_P)), spec((256, S5_P)),
                  spec((S5_P, 256)), spec((S5_P, 256)), spec((S5_SUB, S5_P)), spec((S5_SUB, S5_P)),
                  spec((8, S5_P))],
        out_specs=spec((r, 256)),
        out_shape=jax.ShapeDtypeStruct((2, S5_G, r, 256), BF16),
        scratch_shapes=[pltpu.VMEM((r, S5_P), F32), pltpu.VMEM((r, S5_P), F32),
                        pltpu.VMEM((rb, S5_P), F32), pltpu.VMEM((rb, S5_P), F32),
                        pltpu.VMEM((rb, S5_P), F32), pltpu.VMEM((rb, S5_P), F32)],
        compiler_params=_cparams(("parallel", "parallel")),
        name="s5_scan",
    )(t, tabs['toe'], tabs['wst_re'], tabs['wst_im'], tabs['wi_re'], tabs['wi_im'],
      tabs['pw_re'], tabs['pw_im'], tabs['a256'])
    y = y.reshape(2, S5_G, S5_SUB, rb, S5_SUB, S5_H)[:, :, :, :b * nblk]
    y = y.reshape(2, S5_G, S5_SUB, b, nblk, S5_SUB, S5_H).transpose(0, 3, 4, 2, 5, 1, 6).reshape(2, b, nt, S5_W)
    nc = nt - n_lat
    y_f = jnp.concatenate([y[0, :, nc:], y[0, :, :nc]], axis=1)
    y_r = jnp.concatenate([y[1, :, nc:][:, ::-1], y[1, :, :nc][:, ::-1]], axis=1)
    return y_f, y_r


def _ssd_kernel(xbc_ref, prev_ref, next_ref, dt_ref, dtt_ref, cw_ref, cb_ref, bias_ref, biast_ref,
                a_ref, at_ref, dskip_ref, y_ref, h_ref, *, rev, nl):
    step = pl.program_id(1)
    chunk = jnp.where(step == 0, nl, (nl - step) if rev else (step - 1))
    k0 = SSD_NH if rev else 0

    @pl.when(step == 0)
    def _():
        h_ref[...] = jnp.zeros_like(h_ref)

    x = xbc_ref[0].astype(F32)
    is_lat = chunk != nl
    p_ok = jnp.logical_and(is_lat, chunk > 0).astype(F32)
    n_ok = jnp.logical_and(is_lat, chunk < nl - 1).astype(F32)
    prev_row = prev_ref[0, 7:8, :].astype(F32) * p_ok
    next_row = next_ref[0, 0:1, :].astype(F32) * n_ok
    ridx = lax.broadcasted_iota(jnp.int32, (CHUNK, 1), 0)
    xm = jnp.where(ridx == 0, prev_row, pltpu.roll(x, 1, 0))
    xp = jnp.where(ridx == CHUNK - 1, next_row, pltpu.roll(x, CHUNK - 1, 0))
    conv = cw_ref[0:1, :] * xm + cw_ref[1:2, :] * x + cw_ref[2:3, :] * xp + cb_ref[...]
    act = _silu(conv)
    xs = act[:, :SSD_W]
    bm = act[:, SSD_W:SSD_W + SSD_NG * SSD_N].astype(BF16)
    cm = act[:, SSD_W + SSD_NG * SSD_N:].astype(BF16)

    dt = _softplus(dt_ref[0, :, k0:k0 + SSD_NH].astype(F32) + bias_ref[...])
    dtt = _softplus(dtt_ref[0, k0:k0 + SSD_NH, :] + biast_ref[...])
    da = dt * a_ref[...]
    dat = dtt * at_ref[...]
    li = lax.broadcasted_iota(jnp.int32, (CHUNK, CHUNK), 0)
    si = lax.broadcasted_iota(jnp.int32, (CHUNK, CHUNK), 1)
    mask = (si >= li) if rev else (si <= li)
    tri = mask.astype(F32)
    tri_t = ((li >= si) if rev else (li <= si)).astype(F32)
    hp = lax.Precision.HIGHEST
    cum = jnp.dot(tri, da, preferred_element_type=F32, precision=hp)
    cum_t = jnp.dot(dat, tri_t, preferred_element_type=F32, precision=hp)
    total = jnp.sum(da, axis=0, keepdims=True)
    e_cum = jnp.exp(cum)
    w_end = jnp.exp(total - cum) * dt
    e_tot = jnp.exp(total)

    for g in range(SSD_NG):
        c_g = cm[:, g * SSD_N:(g + 1) * SSD_N]
        b_g = bm[:, g * SSD_N:(g + 1) * SSD_N]
        cb = lax.dot_general(c_g, b_g, (((1,), (1,)), ((), ())), preferred_element_type=F32)
        for r in range(SSD_NH // SSD_NG):
            hh = g * (SSD_NH // SSD_NG) + r
            x_h = xs[:, hh * SSD_HD:(hh + 1) * SSD_HD]
            seg = cum[:, hh:hh + 1] - cum_t[hh:hh + 1, :]
            w = (cb * jnp.exp(jnp.where(mask, seg, -jnp.inf))).astype(BF16)
            y = jnp.dot(w, (x_h * dt[:, hh:hh + 1]).astype(BF16), preferred_element_type=F32)
            h_old = h_ref[hh]
            y += jnp.dot(c_g, h_old.astype(BF16), preferred_element_type=F32) * e_cum[:, hh:hh + 1]
            st = lax.dot_general(b_g, (x_h * w_end[:, hh:hh + 1]).astype(BF16), (((0,), (0,)), ((), ())),
                                 preferred_element_type=F32)
            h_ref[hh] = e_tot[:, hh:hh + 1] * h_old + st
            if not rev:
                y += dskip_ref[:, hh * SSD_HD:(hh + 1) * SSD_HD] * x_h
            y_ref[0, :, hh * SSD_HD:(hh + 1) * SSD_HD] = y.astype(y_ref.dtype)


def _ssd_direction(z_all, dtt, conv_w, conv_b, dt_bias, a_log, d_skip, rev):
    b, nt, _ = z_all.shape
    nc = nt // CHUNK
    nl = nc - 1
    k = 1 if rev else 0

    def chunk_of(s):
        return jnp.where(s == 0, nl, (nl - s) if rev else (s - 1))

    a = -jnp.exp(a_log[k].astype(F32))
    bias = dt_bias[k].astype(F32)
    small = lambda shape: pl.BlockSpec(shape, lambda bi, s: (0,) * len(shape))
    return pl.pallas_call(
        functools.partial(_ssd_kernel, rev=rev, nl=nl),
        grid=(b, nc),
        in_specs=[pl.BlockSpec((1, CHUNK, 1024), lambda bi, s: (bi, chunk_of(s), C_XBC // 1024)),
                  pl.BlockSpec((1, 8, 1024), lambda bi, s: (bi, jnp.maximum(chunk_of(s) * 32 - 1, 0), C_XBC // 1024)),
                  pl.BlockSpec((1, 8, 1024), lambda bi, s: (bi, jnp.minimum((chunk_of(s) + 1) * 32, nt // 8 - 1), C_XBC // 1024)),
                  pl.BlockSpec((1, CHUNK, 128), lambda bi, s: (bi, chunk_of(s), C_DT // 128)),
                  pl.BlockSpec((1, 16, CHUNK), lambda bi, s: (bi, 0, chunk_of(s))),
                  small((3, 1024)), small((1, 1024)), small((1, SSD_NH)), small((SSD_NH, 1)),
                  small((1, SSD_NH)), small((SSD_NH, 1)), small((1, SSD_W))],
        out_specs=pl.BlockSpec((1, CHUNK, SSD_W), lambda bi, s: (bi, chunk_of(s), 0)),
        out_shape=jax.ShapeDtypeStruct((b, nt, SSD_W), F32),
        scratch_shapes=[pltpu.VMEM((SSD_NH, SSD_N, SSD_HD), F32)],
        compiler_params=_cparams(("parallel", "arbitrary")),
        name="ssd_rev" if rev else "ssd_fwd",
    )(z_all, z_all, z_all, z_all, dtt, conv_w.astype(F32), conv_b.astype(F32).reshape(1, 1024),
      bias.reshape(1, SSD_NH), bias.reshape(SSD_NH, 1), a.reshape(1, SSD_NH), a.reshape(SSD_NH, 1),
      jnp.repeat(d_skip.astype(F32), SSD_HD).reshape(1, SSD_W))


def _rope(y, cos, sin, width):
    lane = lax.broadcasted_iota(jnp.int32, (1, width), 1)
    first = (lane & (ATT_HD - 1)) < (ATT_HD // 2)
    rot = jnp.where(first, -pltpu.roll(y, width - ATT_HD // 2, 1), pltpu.roll(y, ATT_HD // 2, 1))
    return y * cos + rot * sin


def _qkv_kernel(q_ref, k_ref, v_ref, cos_ref, sin_ref, qn_ref, kn_ref, ones_ref, qt_ref, ko_ref, vt_ref):
    cos = cos_ref[...]
    sin = sin_ref[...]
    tp = cos.shape[0]
    q = q_ref[0].astype(F32)
    ssq = jnp.dot((q * q).astype(BF16), ones_ref[...], preferred_element_type=F32)
    q = q * lax.rsqrt(ssq * (1.0 / ATT_HD) + EPS) * qn_ref[...]
    q = _rope(q, jnp.concatenate([cos] * 4, axis=1), jnp.concatenate([sin] * 4, axis=1), ATT_NH * ATT_HD)
    q = q * (ATT_HD ** -0.5)
    for g in range(ATT_NKV):
        qt_ref[0, g] = q[:, g * 256:(g + 1) * 256].T.astype(qt_ref.dtype)
    k = k_ref[0].astype(F32)
    ssk = jnp.dot((k * k).astype(BF16), ones_ref[0:128, 0:128], preferred_element_type=F32)
    k = k * lax.rsqrt(ssk * (1.0 / ATT_HD) + EPS) * kn_ref[...]
    k = _rope(k, cos, sin, ATT_NKV * ATT_HD)
    vt = v_ref[0].astype(F32).T
    pad_row = lax.broadcasted_iota(jnp.int32, (VT_ROWS - ATT_HD, tp), 0)
    pad = jnp.where(pad_row == 0, 1.0, 0.0).astype(vt_ref.dtype)
    for g in range(ATT_NKV):
        ko_ref[0, g] = k[:, g * ATT_HD:(g + 1) * ATT_HD].astype(ko_ref.dtype)
        vt_ref[0, g, 0:ATT_HD, :] = vt[g * ATT_HD:(g + 1) * ATT_HD].astype(vt_ref.dtype)
        vt_ref[0, g, ATT_HD:VT_ROWS, :] = pad


def _qkv_prep(z_all, cos, sin, q_norm, k_norm, tp):
    b, nt, _ = z_all.shape
    head = jnp.arange(ATT_NH * ATT_HD) // ATT_HD
    ones = (head[:, None] == head[None, :]).astype(BF16)
    return pl.pallas_call(
        _qkv_kernel,
        grid=(b, nt // tp),
        in_specs=[pl.BlockSpec((1, tp, 512), lambda bi, i: (bi, i, C_Q // 512)),
                  pl.BlockSpec((1, tp, 128), lambda bi, i: (bi, i, C_K // 128)),
                  pl.BlockSpec((1, tp, 128), lambda bi, i: (bi, i, C_V // 128)),
                  pl.BlockSpec((tp, 128), lambda bi, i: (i, 0)),
                  pl.BlockSpec((tp, 128), lambda bi, i: (i, 0)),
                  pl.BlockSpec((1, 512), lambda bi, i: (0, 0)),
                  pl.BlockSpec((1, 128), lambda bi, i: (0, 0)),
                  pl.BlockSpec((512, 512), lambda bi, i: (0, 0))],
        out_specs=[pl.BlockSpec((1, ATT_NKV, 256, tp), lambda bi, i: (bi, 0, 0, i)),
                   pl.BlockSpec((1, ATT_NKV, tp, ATT_HD), lambda bi, i: (bi, 0, i, 0)),
                   pl.BlockSpec((1, ATT_NKV, VT_ROWS, tp), lambda bi, i: (bi, 0, 0, i))],
        out_shape=[jax.ShapeDtypeStruct((b, ATT_NKV, 256, nt), BF16),
                   jax.ShapeDtypeStruct((b, ATT_NKV, nt, ATT_HD), BF16),
                   jax.ShapeDtypeStruct((b, ATT_NKV, VT_ROWS, nt), BF16)],
        compiler_params=_cparams(("parallel", "parallel")),
        name="qkv_prep",
    )(z_all, z_all, z_all, cos, sin, jnp.tile(q_norm.astype(F32), ATT_NH).reshape(1, 512),
      jnp.tile(k_norm.astype(F32), ATT_NKV).reshape(1, 128), ones)


def _rope_tables(n_lat, nt):
    half = ATT_HD // 4
    inv_freq = ROPE_THETA ** (-jnp.arange(half, dtype=F32) / half)
    pos = jnp.arange(n_lat)
    row = (pos // GRID_W).astype(F32)
    col = (pos % GRID_W).astype(F32)
    ang = jnp.concatenate([row[:, None] * inv_freq, col[:, None] * inv_freq], axis=-1)
    cos = jnp.concatenate([jnp.cos(ang), jnp.ones((nt - n_lat, 2 * half), F32)], axis=0)
    sin = jnp.concatenate([jnp.sin(ang), jnp.zeros((nt - n_lat, 2 * half), F32)], axis=0)
    return jnp.tile(cos, (1, 4)), jnp.tile(sin, (1, 4))


def _attn_kernel(qt_ref, k_ref, vt_ref, o_ref, m_ref, acc_ref):
    j = pl.program_id(3)

    @pl.when(j == 0)
    def _():
        m_ref[...] = jnp.full_like(m_ref, -jnp.inf)
        acc_ref[...] = jnp.zeros_like(acc_ref)

    k = k_ref[0, 0]
    vt = vt_ref[0, 0]
    for h in range(ATT_GRP):
        qh = qt_ref[0, 0, h * ATT_HD:(h + 1) * ATT_HD, :]
        s = jnp.dot(k, qh, preferred_element_type=F32)
        m_old = m_ref[h]
        m_new = jnp.maximum(m_old, jnp.max(s, axis=0, keepdims=True))
        p = jnp.exp(s - m_new).astype(BF16)
        acc_ref[h] = jnp.exp(m_old - m_new) * acc_ref[h] + jnp.dot(vt, p, preferred_element_type=F32)
        m_ref[h] = m_new

    @pl.when(j == pl.num_programs(3) - 1)
    def _():
        outs = []
        for h in range(ATT_GRP):
            acc = acc_ref[h]
            outs.append(acc[0:ATT_HD] / acc[ATT_HD:ATT_HD + 1])
        o_ref[0] = jnp.concatenate(outs, axis=0).T.astype(o_ref.dtype)


def _attention(qt, k, vt, nq, tq, tk):
    b = qt.shape[0]
    nk = k.shape[2]
    return pl.pallas_call(
        _attn_kernel,
        grid=(b, ATT_NKV, nq // tq, nk // tk),
        in_specs=[pl.BlockSpec((1, 1, 256, tq), lambda bi, g, i, j: (bi, g, 0, i)),
                  pl.BlockSpec((1, 1, tk, ATT_HD), lambda bi, g, i, j: (bi, g, j, 0)),
                  pl.BlockSpec((1, 1, VT_ROWS, tk), lambda bi, g, i, j: (bi, g, 0, j))],
        out_specs=pl.BlockSpec((1, tq, 256), lambda bi, g, i, j: (bi, i, g)),
        out_shape=jax.ShapeDtypeStruct((b, nq, ATT_NH * ATT_HD), BF16),
        scratch_shapes=[pltpu.VMEM((ATT_GRP, 1, tq), F32), pltpu.VMEM((ATT_GRP, VT_ROWS, tq), F32)],
        compiler_params=_cparams(("parallel", "parallel", "parallel", "arbitrary")),
        name="attention",
    )(qt, k, vt)


def _merge_kernel(x_ref, mg_ref, u_ref, ug_ref, z_ref, ag_ref, y5f_ref, y5r_ref, ysf_ref, ysr_ref, yc_ref,
                  mod_ref, s5d_ref, nrm_ref, pn_ref, glu_ref, wa_ref, wb_ref, wc_ref, wo_ref, o_ref, *, n_lat, tm):
    i = pl.program_id(1)
    f = lambda r: r[0].astype(F32)
    u = f(u_ref)
    ya = jax.nn.gelu(f(y5f_ref) + f(y5r_ref) + s5d_ref[...] * u, approximate=True)
    ya = ya * jax.nn.sigmoid(jnp.dot(ya.astype(BF16), glu_ref[...], preferred_element_type=F32))
    ya = ya * _silu(f(ug_ref))
    yb = (f(ysf_ref) + f(ysr_ref)) * _silu(f(z_ref))
    yb = yb * lax.rsqrt(jnp.mean(yb * yb, axis=-1, keepdims=True) + EPS) * nrm_ref[...]
    yc = f(yc_ref) * _silu(f(ag_ref))
    gates = jax.nn.sigmoid(f(mg_ref))
    mix = gates[:, 0:D] * jnp.dot(ya.astype(BF16), wa_ref[...], preferred_element_type=F32)
    mix += gates[:, D:2 * D] * jnp.dot(yb.astype(BF16), wb_ref[...], preferred_element_type=F32)
    mix += gates[:, 2 * D:3 * D] * jnp.dot(yc.astype(BF16), wc_ref[...], preferred_element_type=F32)
    out = jnp.dot(mix.astype(BF16), wo_ref[...], preferred_element_type=F32)
    out = out * lax.rsqrt(jnp.mean(out * out, axis=-1, keepdims=True) + EPS) * pn_ref[...]
    m = mod_ref[0]
    row = i * tm + lax.broadcasted_iota(jnp.int32, (tm, 1), 0)
    gate = jnp.where(row >= n_lat, m[5:6], m[2:3])
    o_ref[0] = x_ref[0] + gate * out


def _merge(x_all, z_all, y5f, y5r, ysf, ysr, yc, modv, s5_d, ssd_norm, post_norm, glu, wa, wb, wc, wo,
           n_lat, n_rows, tm):
    b = x_all.shape[0]
    tok = lambda w, c: pl.BlockSpec((1, tm, w), lambda bi, i: (bi, i, c))
    full = lambda shape: pl.BlockSpec(shape, lambda bi, i: (0,) * len(shape))
    return pl.pallas_call(
        functools.partial(_merge_kernel, n_lat=n_lat, tm=tm),
        grid=(b, n_rows // tm),
        in_specs=[tok(D, 0), tok(3 * D, 0), tok(512, C_U // 512), tok(512, C_UG // 512), tok(512, C_Z // 512),
                  tok(512, C_AG // 512), tok(512, 0), tok(512, 0), tok(512, 0), tok(512, 0), tok(512, 0),
                  pl.BlockSpec((1, 8, D), lambda bi, i: (bi, 0, 0)),
                  full((1, 512)), full((1, 512)), full((1, D)),
                  full((512, 512)), full((512, D)), full((512, D)), full((512, D)), full((D, D))],
        out_specs=tok(D, 0),
        out_shape=jax.ShapeDtypeStruct((b, n_rows, D), F32),
        compiler_params=_cparams(("parallel", "parallel")),
        name="merge",
    )(x_all, z_all, z_all, z_all, z_all, z_all, y5f, y5r, ysf, ysr, yc, modv,
      s5_d.astype(F32).reshape(1, 512), ssd_norm.astype(F32).reshape(1, 512), post_norm.astype(F32).reshape(1, D),
      glu.astype(BF16), wa.astype(BF16), wb.astype(BF16), wc.astype(BF16), wo.astype(BF16))


def _largest_divisor(n, cands):
    for c in cands:
        if n % c == 0:
            return c
    raise ValueError(f"no tile in {cands} divides {n}")


def _layer(x_all, mod_l, cos, sin, n_lat, with_ctx, pre_norm, post_norm, w_in, s5_lam_re, s5_lam_im, s5_log_dt,
           s5_b_re, s5_b_im, s5_c_re, s5_c_im, s5_d, s5_glu, ssd_conv_w, ssd_conv_b, ssd_dt_bias, ssd_a_log, ssd_d,
           ssd_norm, att_q_norm, att_k_norm, w_branch_a, w_branch_b, w_branch_c, w_out):
    bsz, nt, _ = x_all.shape
    n_ctx = nt - n_lat
    tm_proj = CHUNK * _largest_divisor(nt // CHUNK, (5, 4, 3, 2, 1))
    tq = _largest_divisor(n_lat, (512, 256))
    ml = mod_l[:bsz].reshape(bsz, 3, D)
    mc = jnp.broadcast_to(mod_l[bsz].reshape(1, 3, D), (bsz, 3, D))
    modv = jnp.concatenate([ml, mc, jnp.zeros((bsz, 2, D), F32)], axis=1)
    z_all = _in_proj(x_all, modv, pre_norm.astype(F32), _pack_w_in(w_in), n_lat, tm_proj)

    tabs = _s5_tables(s5_lam_re, s5_lam_im, s5_log_dt, s5_b_re, s5_b_im, s5_c_re, s5_c_im)
    y5f, y5r = _s5_scan(z_all[:, :, C_U:C_U + S5_W], tabs, n_lat)

    dtt = z_all[:, :, C_DT:C_DT + 2 * SSD_NH].astype(F32).transpose(0, 2, 1)
    ssd_args = (z_all, dtt, ssd_conv_w, ssd_conv_b, ssd_dt_bias, ssd_a_log, ssd_d)
    ysf = _ssd_direction(*ssd_args, rev=False)
    ysr = _ssd_direction(*ssd_args, rev=True)

    qt, kk, vt = _qkv_prep(z_all, cos, sin, att_q_norm, att_k_norm, CHUNK)
    yc = _attention(qt, kk, vt, n_lat, tq, tm_proj)
    if with_ctx:
        yc_ctx = _attention(qt[..., n_lat:], kk[:, :, n_lat:], vt[..., n_lat:], n_ctx, n_ctx, n_ctx)
        yc = jnp.concatenate([yc, yc_ctx], axis=1)
    n_rows = nt if with_ctx else n_lat
    return _merge(x_all, z_all, y5f, y5r, ysf, ysr, yc, modv, s5_d, ssd_norm, post_norm,
                  s5_glu, w_branch_a, w_branch_b, w_branch_c, w_out, n_lat, n_rows, CHUNK)


def kernel(x, c, ctx, c_ctx, mod_w, mod_b, pre_norm, post_norm, w_in, s5_lam_re, s5_lam_im, s5_log_dt, s5_b_re, s5_b_im, s5_c_re, s5_c_im, s5_d, s5_glu, ssd_conv_w, ssd_conv_b, ssd_dt_bias, ssd_a_log, ssd_d, ssd_norm, att_q_norm, att_k_norm, w_branch_a, w_branch_b, w_branch_c, w_out):
    bsz, n_lat, _ = x.shape
    n_ctx = ctx.shape[1]
    depth = mod_w.shape[0]
    nt = n_lat + n_ctx
    assert n_lat % CHUNK == 0 and n_ctx == CHUNK and n_lat % GRID_W == 0 and bsz + 1 <= 8
    c8 = jnp.concatenate([c.astype(F32), c_ctx.astype(F32)[None], jnp.zeros((8 - bsz - 1, D), F32)], axis=0)
    mod = _modulation(c8, mod_w.astype(F32), mod_b.astype(F32))
    cos, sin = _rope_tables(n_lat, nt)
    x_all = jnp.concatenate([x.astype(F32), ctx.astype(F32)], axis=1)
    per_layer = (pre_norm, post_norm, w_in, s5_lam_re, s5_lam_im, s5_log_dt, s5_b_re, s5_b_im, s5_c_re, s5_c_im,
                 s5_d, s5_glu, ssd_conv_w, ssd_conv_b, ssd_dt_bias, ssd_a_log, ssd_d, ssd_norm, att_q_norm,
                 att_k_norm, w_branch_a, w_branch_b, w_branch_c, w_out)
    for l in range(depth):
        x_all = _layer(x_all, mod[l], cos, sin, n_lat, l < depth - 1, *[p[l] for p in per_layer])
    return x_all.astype(x.dtype)
```

```python
import functools

import jax
import jax.numpy as jnp
from jax import lax
from jax.experimental import pallas as pl
from jax.experimental.pallas import tpu as pltpu

F32 = jnp.float32
BF16 = jnp.bfloat16

D = 1024
EPS = 1e-6
GRID_W = 64
S5_G, S5_H, S5_P = 32, 16, 64
S5_W = S5_G * S5_H
S5_SUB = 16
SSD_W, SSD_HD, SSD_NH, SSD_NG, SSD_N = 512, 64, 8, 2, 128
CHUNK = 256
ATT_HD, ATT_NH, ATT_NKV = 64, 8, 2
ATT_GRP = ATT_NH // ATT_NKV
ROPE_THETA = 10000.0
LOG2E = 1.4426950408889634
VT_ROWS = 80

C_MG, C_U, C_UG, C_XBC, C_Z, C_Q, C_AG, C_K, C_V, C_DT = 0, 3072, 3584, 4096, 5120, 5632, 6144, 6656, 6784, 6912
ZW = 7040
PROJ_TN = 640
VMEM_LIMIT = 56 * 1024 * 1024


def _cparams(sem):
    return pltpu.CompilerParams(dimension_semantics=sem, vmem_limit_bytes=VMEM_LIMIT)


def _silu(x):
    return x * jax.nn.sigmoid(x)


def _softplus(x):
    return jnp.maximum(x, 0.0) + jnp.log1p(jnp.exp(-jnp.abs(x)))


def _mod_kernel(c_ref, w_ref, b_ref, o_ref):
    c = c_ref[...]
    o_ref[0] = jnp.dot(_silu(c), w_ref[0], preferred_element_type=F32,
                       precision=lax.Precision.HIGHEST) + b_ref[0]


def _modulation(c8, mod_w, mod_b):
    nl = mod_w.shape[0]
    return pl.pallas_call(
        _mod_kernel,
        grid=(nl, 3),
        in_specs=[pl.BlockSpec((8, D), lambda l, j: (0, 0)),
                  pl.BlockSpec((1, D, D), lambda l, j: (l, 0, j)),
                  pl.BlockSpec((1, 1, D), lambda l, j: (l, 0, j))],
        out_specs=pl.BlockSpec((1, 8, D), lambda l, j: (l, 0, j)),
        out_shape=jax.ShapeDtypeStruct((nl, 8, 3 * D), F32),
        compiler_params=_cparams(("parallel", "parallel")),
        name="modulation",
    )(c8, mod_w, mod_b.reshape(nl, 1, 3 * D))


def _proj_kernel(x_ref, mod_ref, pn_ref, w_ref, o_ref, h_ref, *, n_lat, tm):
    i = pl.program_id(1)

    @pl.when(pl.program_id(2) == 0)
    def _():
        x = x_ref[0]
        y = x * lax.rsqrt(jnp.mean(x * x, axis=-1, keepdims=True) + EPS) * pn_ref[...]
        m = mod_ref[0]
        h_l = y * (1.0 + m[1:2]) + m[0:1]
        h_c = y * (1.0 + m[4:5]) + m[3:4]
        row = i * tm + lax.broadcasted_iota(jnp.int32, (tm, 1), 0)
        h_ref[...] = jnp.where(row >= n_lat, h_c, h_l).astype(BF16)

    o_ref[0] = jnp.dot(h_ref[...], w_ref[...], preferred_element_type=F32).astype(o_ref.dtype)


def _in_proj(x_all, modv, pre_norm, w, n_lat, tm):
    b, nt, _ = x_all.shape
    return pl.pallas_call(
        functools.partial(_proj_kernel, n_lat=n_lat, tm=tm),
        grid=(b, nt // tm, ZW // PROJ_TN),
        in_specs=[pl.BlockSpec((1, tm, D), lambda bi, i, j: (bi, i, 0)),
                  pl.BlockSpec((1, 8, D), lambda bi, i, j: (bi, 0, 0)),
                  pl.BlockSpec((1, D), lambda bi, i, j: (0, 0)),
                  pl.BlockSpec((D, PROJ_TN), lambda bi, i, j: (0, j))],
        out_specs=pl.BlockSpec((1, tm, PROJ_TN), lambda bi, i, j: (bi, i, j)),
        out_shape=jax.ShapeDtypeStruct((b, nt, ZW), BF16),
        scratch_shapes=[pltpu.VMEM((tm, D), BF16)],
        compiler_params=_cparams(("parallel", "parallel", "arbitrary")),
        name="in_proj",
    )(x_all, modv, pre_norm.reshape(1, D), w)


def _pack_w_in(w):
    u, ug, xbc, z, dt, q, k, v, ag, mg = jnp.split(
        w, [512, 1024, 2048, 2560, 2576, 3088, 3216, 3344, 3856], axis=-1)
    dt = jnp.pad(dt, ((0, 0), (0, 128 - dt.shape[-1])))
    return jnp.concatenate([mg, u, ug, xbc, z, q, ag, k, v, dt], axis=-1).astype(BF16)


def _s5_tables(lam_re, lam_im, log_dt, b_re, b_im, c_re, c_im):
    lam = lax.complex(lam_re.astype(F32), lam_im.astype(F32))
    ladt = lam * jnp.exp(log_dt.astype(F32))[..., None]
    b_bar = ((jnp.exp(ladt) - 1.0) / lam)[..., None] * lax.complex(b_re.astype(F32), b_im.astype(F32))
    cm = lax.complex(c_re.astype(F32), c_im.astype(F32))
    n = jnp.arange(S5_SUB, dtype=F32)

    def powers(e):
        return jnp.exp(e[:, None, None, None] * ladt[None])

    kd = jnp.real(jnp.einsum('kghp,dkgp,kgpj->dkghj', cm, powers(n), b_bar))
    s_idx = jnp.arange(S5_SUB)[:, None]
    t_idx = jnp.arange(S5_SUB)[None, :]
    toe = kd[jnp.clip(t_idx - s_idx, 0, S5_SUB - 1)]
    toe = jnp.where((t_idx >= s_idx)[:, :, None, None, None, None], toe, 0.0)
    toe = toe.transpose(2, 3, 0, 5, 1, 4).reshape(2, S5_G, 256, 256)
    wst = jnp.einsum('skgp,kgpj->kgsjp', powers(S5_SUB - 1.0 - n), b_bar).reshape(2, S5_G, 256, S5_P)
    wi = jnp.einsum('kghp,tkgp->kgpth', cm, powers(n + 1.0)).reshape(2, S5_G, S5_P, 256)
    pw = powers(S5_SUB * n).transpose(1, 2, 0, 3)
    a256 = jnp.exp(float(CHUNK) * ladt)[:, :, None, :]
    a256 = jnp.concatenate([jnp.real(a256), jnp.imag(a256), jnp.zeros((2, S5_G, 6, S5_P), F32)], axis=2)
    return dict(toe=toe.astype(BF16),
                wst_re=jnp.real(wst).astype(BF16), wst_im=jnp.imag(wst).astype(BF16),
                wi_re=jnp.real(wi).astype(BF16), wi_im=(-jnp.imag(wi)).astype(BF16),
                pw_re=jnp.real(pw), pw_im=jnp.imag(pw), a256=a256)


def _s5_kernel(u_ref, toe_ref, wre_ref, wim_ref, wire_ref, wiim_ref, pwre_ref, pwim_ref, a256_ref, y_ref,
               hre_ref, him_ref, ere_ref, eim_ref, gre_ref, gim_ref, *, rb, nblk, nb):
    u = u_ref[0, 0]
    s_re = jnp.dot(u, wre_ref[0, 0], preferred_element_type=F32)
    s_im = jnp.dot(u, wim_ref[0, 0], preferred_element_type=F32)
    a_re = pwre_ref[0, 0, 1:2, :]
    a_im = pwim_ref[0, 0, 1:2, :]
    p_re = jnp.zeros((rb, S5_P), F32)
    p_im = jnp.zeros((rb, S5_P), F32)
    for j in range(S5_SUB):
        hre_ref[j * rb:(j + 1) * rb, :] = p_re
        him_ref[j * rb:(j + 1) * rb, :] = p_im
        sr = s_re[j * rb:(j + 1) * rb]
        si = s_im[j * rb:(j + 1) * rb]
        p_re, p_im = a_re * p_re - a_im * p_im + sr, a_re * p_im + a_im * p_re + si
    ere_ref[...] = p_re
    eim_ref[...] = p_im
    gre_ref[...] = jnp.zeros_like(gre_ref)
    gim_ref[...] = jnp.zeros_like(gim_ref)
    b_re = a256_ref[0, 0, 0:1, :]
    b_im = a256_ref[0, 0, 1:2, :]

    def block_step(t, carry):
        for b in range(nb):
            r = b * nblk + t
            g_re = gre_ref[pl.ds(r, 1), :]
            g_im = gim_ref[pl.ds(r, 1), :]
            gre_ref[pl.ds(r + 1, 1), :] = b_re * g_re - b_im * g_im + ere_ref[pl.ds(r, 1), :]
            gim_ref[pl.ds(r + 1, 1), :] = b_re * g_im + b_im * g_re + eim_ref[pl.ds(r, 1), :]
        return carry

    lax.fori_loop(0, nblk - 1, block_step, 0)
    g_re = gre_ref[...]
    g_im = gim_ref[...]
    for j in range(S5_SUB):
        q_re = pwre_ref[0, 0, j:j + 1, :]
        q_im = pwim_ref[0, 0, j:j + 1, :]
        hre_ref[j * rb:(j + 1) * rb, :] += q_re * g_re - q_im * g_im
        him_ref[j * rb:(j + 1) * rb, :] += q_re * g_im + q_im * g_re
    y = jnp.dot(u, toe_ref[0, 0], preferred_element_type=F32)
    y += jnp.dot(hre_ref[...].astype(BF16), wire_ref[0, 0], preferred_element_type=F32)
    y += jnp.dot(him_ref[...].astype(BF16), wiim_ref[0, 0], preferred_element_type=F32)
    y_ref[0, 0] = y.astype(y_ref.dtype)


def _s5_scan(u, tabs, n_lat):
    b, nt, _ = u.shape
    nblk = nt // CHUNK
    rb = -(-(b * nblk) // 8) * 8
    ul, uc = u[:, :n_lat], u[:, n_lat:]
    seq = jnp.stack([jnp.concatenate([uc, ul], axis=1),
                     jnp.concatenate([uc[:, ::-1], ul[:, ::-1]], axis=1)])
    t = seq.reshape(2, b, nblk, S5_SUB, S5_SUB, S5_G, S5_H).transpose(0, 5, 3, 1, 2, 4, 6)
    t = t.reshape(2, S5_G, S5_SUB, b * nblk, 256)
    t = jnp.pad(t, ((0, 0), (0, 0), (0, 0), (0, rb - b * nblk), (0, 0))).reshape(2, S5_G, S5_SUB * rb, 256)
    r = S5_SUB * rb

    def spec(shape):
        return pl.BlockSpec((1, 1) + shape, lambda d, g: (d, g, 0, 0))

    y = pl.pallas_call(
        functools.partial(_s5_kernel, rb=rb, nblk=nblk, nb=b),
        grid=(2, S5_G),
        in_specs=[spec((r, 256)), spec((256, 256)), spec((256, S5_P)), spec((256, S5_P)),
                  spec((S5_P, 256)), spec((S5_P, 256)), spec((S5_SUB, S5_P)), spec((S5_SUB, S5_P)),
                  spec((8, S5_P))],
        out_specs=spec((r, 256)),
        out_shape=jax.ShapeDtypeStruct((2, S5_G, r, 256), BF16),
        scratch_shapes=[pltpu.VMEM((r, S5_P), F32), pltpu.VMEM((r, S5_P), F32),
                        pltpu.VMEM((rb, S5_P), F32), pltpu.VMEM((rb, S5_P), F32),
                        pltpu.VMEM((rb, S5_P), F32), pltpu.VMEM((rb, S5_P), F32)],
        compiler_params=_cparams(("parallel", "parallel")),
        name="s5_scan",
    )(t, tabs['toe'], tabs['wst_re'], tabs['wst_im'], tabs['wi_re'], tabs['wi_im'],
      tabs['pw_re'], tabs['pw_im'], tabs['a256'])
    y = y.reshape(2, S5_G, S5_SUB, rb, S5_SUB, S5_H)[:, :, :, :b * nblk]
    y = y.reshape(2, S5_G, S5_SUB, b, nblk, S5_SUB, S5_H).transpose(0, 3, 4, 2, 5, 1, 6).reshape(2, b, nt, S5_W)
    nc = nt - n_lat
    y_f = jnp.concatenate([y[0, :, nc:], y[0, :, :nc]], axis=1)
    y_r = jnp.concatenate([y[1, :, nc:][:, ::-1], y[1, :, :nc][:, ::-1]], axis=1)
    return y_f, y_r


def _ssd_kernel(xbc_ref, prev_ref, next_ref, dt_ref, dtt_ref, cw_ref, cb_ref, bias_ref, biast_ref,
                a_ref, at_ref, dskip_ref, y_ref, h_ref, *, rev, nl):
    step = pl.program_id(1)
    chunk = jnp.where(step == 0, nl, (nl - step) if rev else (step - 1))
    k0 = SSD_NH if rev else 0

    @pl.when(step == 0)
    def _():
        h_ref[...] = jnp.zeros_like(h_ref)

    x = xbc_ref[0].astype(F32)
    is_lat = chunk != nl
    p_ok = jnp.logical_and(is_lat, chunk > 0).astype(F32)
    n_ok = jnp.logical_and(is_lat, chunk < nl - 1).astype(F32)
    prev_row = prev_ref[0, 7:8, :].astype(F32) * p_ok
    next_row = next_ref[0, 0:1, :].astype(F32) * n_ok
    ridx = lax.broadcasted_iota(jnp.int32, (CHUNK, 1), 0)
    xm = jnp.where(ridx == 0, prev_row, pltpu.roll(x, 1, 0))
    xp = jnp.where(ridx == CHUNK - 1, next_row, pltpu.roll(x, CHUNK - 1, 0))
    conv = cw_ref[0:1, :] * xm + cw_ref[1:2, :] * x + cw_ref[2:3, :] * xp + cb_ref[...]
    act = _silu(conv)
    xs = act[:, :SSD_W]
    bm = act[:, SSD_W:SSD_W + SSD_NG * SSD_N].astype(BF16)
    cm = act[:, SSD_W + SSD_NG * SSD_N:].astype(BF16)

    dt = _softplus(dt_ref[0, :, k0:k0 + SSD_NH].astype(F32) + bias_ref[...])
    dtt = _softplus(dtt_ref[0, k0:k0 + SSD_NH, :] + biast_ref[...])
    da = dt * a_ref[...]
    dat = dtt * at_ref[...]
    li = lax.broadcasted_iota(jnp.int32, (CHUNK, CHUNK), 0)
    si = lax.broadcasted_iota(jnp.int32, (CHUNK, CHUNK), 1)
    mask = (si >= li) if rev else (si <= li)
    tri = mask.astype(F32)
    tri_t = ((li >= si) if rev else (li <= si)).astype(F32)
    hp = lax.Precision.HIGHEST
    cum = jnp.dot(tri, da, preferred_element_type=F32, precision=hp)
    cum_t = jnp.dot(dat, tri_t, preferred_element_type=F32, precision=hp)
    total = jnp.sum(da, axis=0, keepdims=True)
    e_cum = jnp.exp(cum)
    w_end = jnp.exp(total - cum) * dt
    e_tot = jnp.exp(total)

    for g in range(SSD_NG):
        c_g = cm[:, g * SSD_N:(g + 1) * SSD_N]
        b_g = bm[:, g * SSD_N:(g + 1) * SSD_N]
        cb = lax.dot_general(c_g, b_g, (((1,), (1,)), ((), ())), preferred_element_type=F32)
        for r in range(SSD_NH // SSD_NG):
            hh = g * (SSD_NH // SSD_NG) + r
            x_h = xs[:, hh * SSD_HD:(hh + 1) * SSD_HD]
            seg = cum[:, hh:hh + 1] - cum_t[hh:hh + 1, :]
            w = (cb * jnp.exp(jnp.where(mask, seg, -jnp.inf))).astype(BF16)
            y = jnp.dot(w, (x_h * dt[:, hh:hh + 1]).astype(BF16), preferred_element_type=F32)
            h_old = h_ref[hh]
            y += jnp.dot(c_g, h_old.astype(BF16), preferred_element_type=F32) * e_cum[:, hh:hh + 1]
            st = lax.dot_general(b_g, (x_h * w_end[:, hh:hh + 1]).astype(BF16), (((0,), (0,)), ((), ())),
                                 preferred_element_type=F32)
            h_ref[hh] = e_tot[:, hh:hh + 1] * h_old + st
            if not rev:
                y += dskip_ref[:, hh * SSD_HD:(hh + 1) * SSD_HD] * x_h
            y_ref[0, :, hh * SSD_HD:(hh + 1) * SSD_HD] = y.astype(y_ref.dtype)


def _ssd_direction(z_all, dtt, conv_w, conv_b, dt_bias, a_log, d_skip, rev):
    b, nt, _ = z_all.shape
    nc = nt // CHUNK
    nl = nc - 1
    k = 1 if rev else 0

    def chunk_of(s):
        return jnp.where(s == 0, nl, (nl - s) if rev else (s - 1))

    a = -jnp.exp(a_log[k].astype(F32))
    bias = dt_bias[k].astype(F32)
    small = lambda shape: pl.BlockSpec(shape, lambda bi, s: (0,) * len(shape))
    return pl.pallas_call(
        functools.partial(_ssd_kernel, rev=rev, nl=nl),
        grid=(b, nc),
        in_specs=[pl.BlockSpec((1, CHUNK, 1024), lambda bi, s: (bi, chunk_of(s), C_XBC // 1024)),
                  pl.BlockSpec((1, 8, 1024), lambda bi, s: (bi, jnp.maximum(chunk_of(s) * 32 - 1, 0), C_XBC // 1024)),
                  pl.BlockSpec((1, 8, 1024), lambda bi, s: (bi, jnp.minimum((chunk_of(s) + 1) * 32, nt // 8 - 1), C_XBC // 1024)),
                  pl.BlockSpec((1, CHUNK, 128), lambda bi, s: (bi, chunk_of(s), C_DT // 128)),
                  pl.BlockSpec((1, 16, CHUNK), lambda bi, s: (bi, 0, chunk_of(s))),
                  small((3, 1024)), small((1, 1024)), small((1, SSD_NH)), small((SSD_NH, 1)),
                  small((1, SSD_NH)), small((SSD_NH, 1)), small((1, SSD_W))],
        out_specs=pl.BlockSpec((1, CHUNK, SSD_W), lambda bi, s: (bi, chunk_of(s), 0)),
        out_shape=jax.ShapeDtypeStruct((b, nt, SSD_W), F32),
        scratch_shapes=[pltpu.VMEM((SSD_NH, SSD_N, SSD_HD), F32)],
        compiler_params=_cparams(("parallel", "arbitrary")),
        name="ssd_rev" if rev else "ssd_fwd",
    )(z_all, z_all, z_all, z_all, dtt, conv_w.astype(F32), conv_b.astype(F32).reshape(1, 1024),
      bias.reshape(1, SSD_NH), bias.reshape(SSD_NH, 1), a.reshape(1, SSD_NH), a.reshape(SSD_NH, 1),
      jnp.repeat(d_skip.astype(F32), SSD_HD).reshape(1, SSD_W))


def _rope(y, cos, sin, width):
    lane = lax.broadcasted_iota(jnp.int32, (1, width), 1)
    first = (lane & (ATT_HD - 1)) < (ATT_HD // 2)
    rot = jnp.where(first, -pltpu.roll(y, width - ATT_HD // 2, 1), pltpu.roll(y, ATT_HD // 2, 1))
    return y * cos + rot * sin


def _qkv_kernel(q_ref, k_ref, v_ref, cos_ref, sin_ref, qn_ref, kn_ref, ones_ref, qt_ref, ko_ref, vt_ref):
    cos = cos_ref[...]
    sin = sin_ref[...]
    tp = cos.shape[0]
    q = q_ref[0].astype(F32)
    ssq = jnp.dot((q * q).astype(BF16), ones_ref[...], preferred_element_type=F32)
    q = q * lax.rsqrt(ssq * (1.0 / ATT_HD) + EPS) * qn_ref[...]
    q = _rope(q, jnp.concatenate([cos] * 4, axis=1), jnp.concatenate([sin] * 4, axis=1), ATT_NH * ATT_HD)
    q = q * (ATT_HD ** -0.5 * LOG2E)
    for g in range(ATT_NKV):
        qt_ref[0, g] = q[:, g * 256:(g + 1) * 256].T.astype(qt_ref.dtype)
    k = k_ref[0].astype(F32)
    ssk = jnp.dot((k * k).astype(BF16), ones_ref[0:128, 0:128], preferred_element_type=F32)
    k = k * lax.rsqrt(ssk * (1.0 / ATT_HD) + EPS) * kn_ref[...]
    k = _rope(k, cos, sin, ATT_NKV * ATT_HD)
    vt = v_ref[0].astype(F32).T
    pad_row = lax.broadcasted_iota(jnp.int32, (VT_ROWS - ATT_HD, tp), 0)
    pad = jnp.where(pad_row == 0, 1.0, 0.0).astype(vt_ref.dtype)
    for g in range(ATT_NKV):
        ko_ref[0, g] = k[:, g * ATT_HD:(g + 1) * ATT_HD].astype(ko_ref.dtype)
        vt_ref[0, g, 0:ATT_HD, :] = vt[g * ATT_HD:(g + 1) * ATT_HD].astype(vt_ref.dtype)
        vt_ref[0, g, ATT_HD:VT_ROWS, :] = pad


def _qkv_prep(z_all, cos, sin, q_norm, k_norm, tp):
    b, nt, _ = z_all.shape
    head = jnp.arange(ATT_NH * ATT_HD) // ATT_HD
    ones = (head[:, None] == head[None, :]).astype(BF16)
    return pl.pallas_call(
        _qkv_kernel,
        grid=(b, nt // tp),
        in_specs=[pl.BlockSpec((1, tp, 512), lambda bi, i: (bi, i, C_Q // 512)),
                  pl.BlockSpec((1, tp, 128), lambda bi, i: (bi, i, C_K // 128)),
                  pl.BlockSpec((1, tp, 128), lambda bi, i: (bi, i, C_V // 128)),
                  pl.BlockSpec((tp, 128), lambda bi, i: (i, 0)),
                  pl.BlockSpec((tp, 128), lambda bi, i: (i, 0)),
                  pl.BlockSpec((1, 512), lambda bi, i: (0, 0)),
                  pl.BlockSpec((1, 128), lambda bi, i: (0, 0)),
                  pl.BlockSpec((512, 512), lambda bi, i: (0, 0))],
        out_specs=[pl.BlockSpec((1, ATT_NKV, 256, tp), lambda bi, i: (bi, 0, 0, i)),
                   pl.BlockSpec((1, ATT_NKV, tp, ATT_HD), lambda bi, i: (bi, 0, i, 0)),
                   pl.BlockSpec((1, ATT_NKV, VT_ROWS, tp), lambda bi, i: (bi, 0, 0, i))],
        out_shape=[jax.ShapeDtypeStruct((b, ATT_NKV, 256, nt), BF16),
                   jax.ShapeDtypeStruct((b, ATT_NKV, nt, ATT_HD), BF16),
                   jax.ShapeDtypeStruct((b, ATT_NKV, VT_ROWS, nt), BF16)],
        compiler_params=_cparams(("parallel", "parallel")),
        name="qkv_prep",
    )(z_all, z_all, z_all, cos, sin, jnp.tile(q_norm.astype(F32), ATT_NH).reshape(1, 512),
      jnp.tile(k_norm.astype(F32), ATT_NKV).reshape(1, 128), ones)


def _rope_tables(n_lat, nt):
    half = ATT_HD // 4
    inv_freq = ROPE_THETA ** (-jnp.arange(half, dtype=F32) / half)
    pos = jnp.arange(n_lat)
    row = (pos // GRID_W).astype(F32)
    col = (pos % GRID_W).astype(F32)
    ang = jnp.concatenate([row[:, None] * inv_freq, col[:, None] * inv_freq], axis=-1)
    cos = jnp.concatenate([jnp.cos(ang), jnp.ones((nt - n_lat, 2 * half), F32)], axis=0)
    sin = jnp.concatenate([jnp.sin(ang), jnp.zeros((nt - n_lat, 2 * half), F32)], axis=0)
    return jnp.tile(cos, (1, 4)), jnp.tile(sin, (1, 4))


def _attn_kernel(qt_ref, k_ref, vt_ref, o_ref, s0_ref, s1_ref, m_ref, acc_ref, *, nk, tk):
    sbuf = (s0_ref, s1_ref)
    m_ref[...] = jnp.full_like(m_ref, -jnp.inf)
    acc_ref[...] = jnp.zeros_like(acc_ref)

    def score(j, h):
        k = k_ref[0, 0, pl.ds(pl.multiple_of(j * tk, tk), tk), :]
        s = jnp.dot(k, qt_ref[0, 0, h * ATT_HD:(h + 1) * ATT_HD, :], preferred_element_type=F32)
        sbuf[h % 2][...] = s
        return jnp.max(s, axis=0, keepdims=True)

    def fold(j, h, tile_max):
        vt = vt_ref[0, 0, :, pl.ds(pl.multiple_of(j * tk, tk), tk)]
        m_old = m_ref[h]
        m_new = jnp.maximum(m_old, tile_max)
        p = jnp.exp2(sbuf[h % 2][...] - m_new).astype(BF16)
        acc_ref[h] = jnp.exp2(m_old - m_new) * acc_ref[h] + jnp.dot(vt, p, preferred_element_type=F32)
        m_ref[h] = m_new

    def key_tile(j, prev_max, first):
        tile_max = score(j, 0)
        if not first:
            fold(j - 1, ATT_GRP - 1, prev_max)
        for h in range(1, ATT_GRP):
            next_max = score(j, h)
            fold(j, h - 1, tile_max)
            tile_max = next_max
        return tile_max

    last_max = key_tile(0, None, True)
    last_max = lax.fori_loop(1, nk, lambda j, pm: key_tile(j, pm, False), last_max)
    fold(nk - 1, ATT_GRP - 1, last_max)

    outs = []
    for h in range(ATT_GRP):
        acc = acc_ref[h]
        outs.append(acc[0:ATT_HD] / acc[ATT_HD:ATT_HD + 1])
    o_ref[0] = jnp.concatenate(outs, axis=0).T.astype(o_ref.dtype)


def _attention(qt, k, vt, nq, tq, tk):
    b = qt.shape[0]
    nkeys = k.shape[2]
    return pl.pallas_call(
        functools.partial(_attn_kernel, nk=nkeys // tk, tk=tk),
        grid=(b, ATT_NKV, nq // tq),
        in_specs=[pl.BlockSpec((1, 1, 256, tq), lambda bi, g, i: (bi, g, 0, i)),
                  pl.BlockSpec((1, 1, nkeys, ATT_HD), lambda bi, g, i: (bi, g, 0, 0)),
                  pl.BlockSpec((1, 1, VT_ROWS, nkeys), lambda bi, g, i: (bi, g, 0, 0))],
        out_specs=pl.BlockSpec((1, tq, 256), lambda bi, g, i: (bi, i, g)),
        out_shape=jax.ShapeDtypeStruct((b, nq, ATT_NH * ATT_HD), BF16),
        scratch_shapes=[pltpu.VMEM((tk, tq), F32), pltpu.VMEM((tk, tq), F32),
                        pltpu.VMEM((ATT_GRP, 1, tq), F32), pltpu.VMEM((ATT_GRP, VT_ROWS, tq), F32)],
        compiler_params=_cparams(("parallel", "parallel", "parallel")),
        name="attention",
    )(qt, k, vt)


def _merge_kernel(x_ref, mg_ref, u_ref, ug_ref, z_ref, ag_ref, y5f_ref, y5r_ref, ysf_ref, ysr_ref, yc_ref,
                  mod_ref, s5d_ref, nrm_ref, pn_ref, glu_ref, wa_ref, wb_ref, wc_ref, wo_ref, o_ref, *, n_lat, tm):
    i = pl.program_id(1)
    f = lambda r: r[0].astype(F32)
    u = f(u_ref)
    ya = jax.nn.gelu(f(y5f_ref) + f(y5r_ref) + s5d_ref[...] * u, approximate=True)
    ya = ya * jax.nn.sigmoid(jnp.dot(ya.astype(BF16), glu_ref[...], preferred_element_type=F32))
    ya = ya * _silu(f(ug_ref))
    yb = (f(ysf_ref) + f(ysr_ref)) * _silu(f(z_ref))
    yb = yb * lax.rsqrt(jnp.mean(yb * yb, axis=-1, keepdims=True) + EPS) * nrm_ref[...]
    yc = f(yc_ref) * _silu(f(ag_ref))
    gates = jax.nn.sigmoid(f(mg_ref))
    mix = gates[:, 0:D] * jnp.dot(ya.astype(BF16), wa_ref[...], preferred_element_type=F32)
    mix += gates[:, D:2 * D] * jnp.dot(yb.astype(BF16), wb_ref[...], preferred_element_type=F32)
    mix += gates[:, 2 * D:3 * D] * jnp.dot(yc.astype(BF16), wc_ref[...], preferred_element_type=F32)
    out = jnp.dot(mix.astype(BF16), wo_ref[...], preferred_element_type=F32)
    out = out * lax.rsqrt(jnp.mean(out * out, axis=-1, keepdims=True) + EPS) * pn_ref[...]
    m = mod_ref[0]
    row = i * tm + lax.broadcasted_iota(jnp.int32, (tm, 1), 0)
    gate = jnp.where(row >= n_lat, m[5:6], m[2:3])
    o_ref[0] = x_ref[0] + gate * out


def _merge(x_all, z_all, y5f, y5r, ysf, ysr, yc, modv, s5_d, ssd_norm, post_norm, glu, wa, wb, wc, wo,
           n_lat, n_rows, tm):
    b = x_all.shape[0]
    tok = lambda w, c: pl.BlockSpec((1, tm, w), lambda bi, i: (bi, i, c))
    full = lambda shape: pl.BlockSpec(shape, lambda bi, i: (0,) * len(shape))
    return pl.pallas_call(
        functools.partial(_merge_kernel, n_lat=n_lat, tm=tm),
        grid=(b, n_rows // tm),
        in_specs=[tok(D, 0), tok(3 * D, 0), tok(512, C_U // 512), tok(512, C_UG // 512), tok(512, C_Z // 512),
                  tok(512, C_AG // 512), tok(512, 0), tok(512, 0), tok(512, 0), tok(512, 0), tok(512, 0),
                  pl.BlockSpec((1, 8, D), lambda bi, i: (bi, 0, 0)),
                  full((1, 512)), full((1, 512)), full((1, D)),
                  full((512, 512)), full((512, D)), full((512, D)), full((512, D)), full((D, D))],
        out_specs=tok(D, 0),
        out_shape=jax.ShapeDtypeStruct((b, n_rows, D), F32),
        compiler_params=_cparams(("parallel", "parallel")),
        name="merge",
    )(x_all, z_all, z_all, z_all, z_all, z_all, y5f, y5r, ysf, ysr, yc, modv,
      s5_d.astype(F32).reshape(1, 512), ssd_norm.astype(F32).reshape(1, 512), post_norm.astype(F32).reshape(1, D),
      glu.astype(BF16), wa.astype(BF16), wb.astype(BF16), wc.astype(BF16), wo.astype(BF16))


def _largest_divisor(n, cands):
    for c in cands:
        if n % c == 0:
            return c
    raise ValueError(f"no tile in {cands} divides {n}")


def _layer(x_all, mod_l, cos, sin, n_lat, with_ctx, pre_norm, post_norm, w_in, s5_lam_re, s5_lam_im, s5_log_dt,
           s5_b_re, s5_b_im, s5_c_re, s5_c_im, s5_d, s5_glu, ssd_conv_w, ssd_conv_b, ssd_dt_bias, ssd_a_log, ssd_d,
           ssd_norm, att_q_norm, att_k_norm, w_branch_a, w_branch_b, w_branch_c, w_out):
    bsz, nt, _ = x_all.shape
    n_ctx = nt - n_lat
    tm_proj = CHUNK * _largest_divisor(nt // CHUNK, (5, 4, 3, 2, 1))
    tq = _largest_divisor(n_lat, (512, 256))
    ml = mod_l[:bsz].reshape(bsz, 3, D)
    mc = jnp.broadcast_to(mod_l[bsz].reshape(1, 3, D), (bsz, 3, D))
    modv = jnp.concatenate([ml, mc, jnp.zeros((bsz, 2, D), F32)], axis=1)
    z_all = _in_proj(x_all, modv, pre_norm.astype(F32), _pack_w_in(w_in), n_lat, tm_proj)

    tabs = _s5_tables(s5_lam_re, s5_lam_im, s5_log_dt, s5_b_re, s5_b_im, s5_c_re, s5_c_im)
    y5f, y5r = _s5_scan(z_all[:, :, C_U:C_U + S5_W], tabs, n_lat)

    dtt = z_all[:, :, C_DT:C_DT + 2 * SSD_NH].astype(F32).transpose(0, 2, 1)
    ssd_args = (z_all, dtt, ssd_conv_w, ssd_conv_b, ssd_dt_bias, ssd_a_log, ssd_d)
    ysf = _ssd_direction(*ssd_args, rev=False)
    ysr = _ssd_direction(*ssd_args, rev=True)

    qt, kk, vt = _qkv_prep(z_all, cos, sin, att_q_norm, att_k_norm, CHUNK)
    yc = _attention(qt, kk, vt, n_lat, tq, tm_proj)
    if with_ctx:
        yc_ctx = _attention(qt[..., n_lat:], kk[:, :, n_lat:], vt[..., n_lat:], n_ctx, n_ctx, n_ctx)
        yc = jnp.concatenate([yc, yc_ctx], axis=1)
    n_rows = nt if with_ctx else n_lat
    return _merge(x_all, z_all, y5f, y5r, ysf, ysr, yc, modv, s5_d, ssd_norm, post_norm,
                  s5_glu, w_branch_a, w_branch_b, w_branch_c, w_out, n_lat, n_rows, CHUNK)


def kernel(x, c, ctx, c_ctx, mod_w, mod_b, pre_norm, post_norm, w_in, s5_lam_re, s5_lam_im, s5_log_dt, s5_b_re, s5_b_im, s5_c_re, s5_c_im, s5_d, s5_glu, ssd_conv_w, ssd_conv_b, ssd_dt_bias, ssd_a_log, ssd_d, ssd_norm, att_q_norm, att_k_norm, w_branch_a, w_branch_b, w_branch_c, w_out):
    bsz, n_lat, _ = x.shape
    n_ctx = ctx.shape[1]
    depth = mod_w.shape[0]
    nt = n_lat + n_ctx
    assert n_lat % CHUNK == 0 and n_ctx == CHUNK and n_lat % GRID_W == 0 and bsz + 1 <= 8
    c8 = jnp.concatenate([c.astype(F32), c_ctx.astype(F32)[None], jnp.zeros((8 - bsz - 1, D), F32)], axis=0)
    mod = _modulation(c8, mod_w.astype(F32), mod_b.astype(F32))
    cos, sin = _rope_tables(n_lat, nt)
    x_all = jnp.concatenate([x.astype(F32), ctx.astype(F32)], axis=1)
    per_layer = (pre_norm, post_norm, w_in, s5_lam_re, s5_lam_im, s5_log_dt, s5_b_re, s5_b_im, s5_c_re, s5_c_im,
                 s5_d, s5_glu, ssd_conv_w, ssd_conv_b, ssd_dt_bias, ssd_a_log, ssd_d, ssd_norm, att_q_norm,
                 att_k_norm, w_branch_a, w_branch_b, w_branch_c, w_out)
    for l in range(depth):
        x_all = _layer(x_all, mod[l], cos, sin, n_lat, l < depth - 1, *[p[l] for p in per_layer])
    return x_all.astype(x.dtype)
```

```python
import functools

import jax
import jax.numpy as jnp
from jax import lax
from jax.experimental import pallas as pl
from jax.experimental.pallas import tpu as pltpu

F32 = jnp.float32
BF16 = jnp.bfloat16

D = 1024
EPS = 1e-6
GRID_W = 64
S5_G, S5_H, S5_P = 32, 16, 64
S5_W = S5_G * S5_H
S5_SUB = 16
SSD_W, SSD_HD, SSD_NH, SSD_NG, SSD_N = 512, 64, 8, 2, 128
CHUNK = 256
ATT_HD, ATT_NH, ATT_NKV = 64, 8, 2
ATT_GRP = ATT_NH // ATT_NKV
ROPE_THETA = 10000.0
LOG2E = 1.4426950408889634
VT_ROWS = 80

C_MG, C_U, C_UG, C_XBC, C_Z, C_Q, C_AG, C_K, C_V, C_DT = 0, 3072, 3584, 4096, 5120, 5632, 6144, 6656, 6784, 6912
ZW = 7168
PROJ_TN = 1024
VMEM_LIMIT = 56 * 1024 * 1024


def _cparams(sem):
    return pltpu.CompilerParams(dimension_semantics=sem, vmem_limit_bytes=VMEM_LIMIT)


def _silu(x):
    return x * jax.nn.sigmoid(x)


def _softplus(x):
    return jnp.maximum(x, 0.0) + jnp.log1p(jnp.exp(-jnp.abs(x)))


def _mod_kernel(c_ref, w_ref, b_ref, o_ref):
    c = c_ref[...]
    o_ref[0] = jnp.dot(_silu(c), w_ref[0], preferred_element_type=F32,
                       precision=lax.Precision.HIGHEST) + b_ref[0]


def _modulation(c8, mod_w, mod_b):
    nl = mod_w.shape[0]
    return pl.pallas_call(
        _mod_kernel,
        grid=(nl, 3),
        in_specs=[pl.BlockSpec((8, D), lambda l, j: (0, 0)),
                  pl.BlockSpec((1, D, D), lambda l, j: (l, 0, j)),
                  pl.BlockSpec((1, 1, D), lambda l, j: (l, 0, j))],
        out_specs=pl.BlockSpec((1, 8, D), lambda l, j: (l, 0, j)),
        out_shape=jax.ShapeDtypeStruct((nl, 8, 3 * D), F32),
        compiler_params=_cparams(("parallel", "parallel")),
        name="modulation",
    )(c8, mod_w, mod_b.reshape(nl, 1, 3 * D))


def _proj_kernel(x_ref, mod_ref, pn_ref, w_ref, o_ref, h_ref, *, n_lat, tm):
    i = pl.program_id(1)

    @pl.when(pl.program_id(2) == 0)
    def _():
        x = x_ref[0]
        y = x * lax.rsqrt(jnp.mean(x * x, axis=-1, keepdims=True) + EPS) * pn_ref[...]
        m = mod_ref[0]
        h_l = y * (1.0 + m[1:2]) + m[0:1]
        h_c = y * (1.0 + m[4:5]) + m[3:4]
        row = i * tm + lax.broadcasted_iota(jnp.int32, (tm, 1), 0)
        h_ref[...] = jnp.where(row >= n_lat, h_c, h_l).astype(BF16)

    o_ref[0] = jnp.dot(h_ref[...], w_ref[...], preferred_element_type=F32).astype(o_ref.dtype)


def _in_proj(x_all, modv, pre_norm, w, n_lat, tm):
    b, nt, _ = x_all.shape
    return pl.pallas_call(
        functools.partial(_proj_kernel, n_lat=n_lat, tm=tm),
        grid=(b, nt // tm, ZW // PROJ_TN),
        in_specs=[pl.BlockSpec((1, tm, D), lambda bi, i, j: (bi, i, 0)),
                  pl.BlockSpec((1, 8, D), lambda bi, i, j: (bi, 0, 0)),
                  pl.BlockSpec((1, D), lambda bi, i, j: (0, 0)),
                  pl.BlockSpec((D, PROJ_TN), lambda bi, i, j: (0, j))],
        out_specs=pl.BlockSpec((1, tm, PROJ_TN), lambda bi, i, j: (bi, i, j)),
        out_shape=jax.ShapeDtypeStruct((b, nt, ZW), BF16),
        scratch_shapes=[pltpu.VMEM((tm, D), BF16)],
        compiler_params=_cparams(("parallel", "parallel", "arbitrary")),
        name="in_proj",
    )(x_all, modv, pre_norm.reshape(1, D), w)


def _pack_w_in(w):
    u, ug, xbc, z, dt, q, k, v, ag, mg = jnp.split(
        w, [512, 1024, 2048, 2560, 2576, 3088, 3216, 3344, 3856], axis=-1)
    dt = jnp.pad(dt, ((0, 0), (0, ZW - C_DT - dt.shape[-1])))
    return jnp.concatenate([mg, u, ug, xbc, z, q, ag, k, v, dt], axis=-1).astype(BF16)


def _s5_tables(lam_re, lam_im, log_dt, b_re, b_im, c_re, c_im):
    lam = lax.complex(lam_re.astype(F32), lam_im.astype(F32))
    ladt = lam * jnp.exp(log_dt.astype(F32))[..., None]
    b_bar = ((jnp.exp(ladt) - 1.0) / lam)[..., None] * lax.complex(b_re.astype(F32), b_im.astype(F32))
    cm = lax.complex(c_re.astype(F32), c_im.astype(F32))
    n = jnp.arange(S5_SUB, dtype=F32)
    fwd = jnp.array([1.0, 0.0], F32)

    def powers(e_fwd, e_rev):
        e = e_fwd[:, None] * fwd + e_rev[:, None] * (1.0 - fwd)
        return jnp.exp(e[:, :, None, None] * ladt[None])

    kd = jnp.real(jnp.einsum('kghp,dkgp,kgpj->dkghj', cm, powers(n, n), b_bar))
    s_idx = jnp.arange(S5_SUB)[:, None]
    t_idx = jnp.arange(S5_SUB)[None, :]
    lag = jnp.stack([t_idx - s_idx, s_idx - t_idx])
    kdt = kd.transpose(1, 0, 2, 3, 4)
    toe = jnp.stack([kdt[k][jnp.clip(lag[k], 0, S5_SUB - 1)] for k in range(2)])
    toe = jnp.where((lag >= 0)[:, :, :, None, None, None], toe, 0.0)
    toe = toe.transpose(0, 3, 1, 5, 2, 4).reshape(2, S5_G, 256, 256)
    wst = jnp.einsum('skgp,kgpj->kgsjp', powers(S5_SUB - 1.0 - n, n), b_bar).reshape(2, S5_G, 256, S5_P)
    wi = jnp.einsum('kghp,tkgp->kgpth', cm, powers(n + 1.0, S5_SUB - n)).reshape(2, S5_G, S5_P, 256)
    pairs = S5_G // 2

    def pair_cols(w):
        w = w.reshape(2, pairs, 2, 256, S5_P)
        z = jnp.zeros_like(w[:, :, 0])
        return jnp.concatenate([jnp.concatenate([w[:, :, 0], z], axis=-1),
                                jnp.concatenate([z, w[:, :, 1]], axis=-1)], axis=-2)

    def pair_rows(w):
        w = w.reshape(2, pairs, 2, S5_P, 256)
        z = jnp.zeros_like(w[:, :, 0])
        return jnp.concatenate([jnp.concatenate([w[:, :, 0], z], axis=-1),
                                jnp.concatenate([z, w[:, :, 1]], axis=-1)], axis=-2)

    def pair_lanes(p):
        return p.transpose(1, 2, 0, 3).reshape(2, pairs, 2, p.shape[0], S5_P).transpose(0, 1, 3, 2, 4).reshape(
            2, pairs, p.shape[0], 2 * S5_P)

    steps = jnp.array([1.0, 2.0, 4.0, 8.0], F32) * S5_SUB
    ak = pair_lanes(powers(steps, steps))
    pw = pair_lanes(powers(S5_SUB * n, S5_SUB * (S5_SUB - 1.0 - n)))
    a256 = pair_lanes(powers(jnp.array([float(CHUNK)], F32), jnp.array([float(CHUNK)], F32)))
    pad4 = jnp.zeros((2, pairs, 4, 2 * S5_P), F32)
    pad7 = jnp.zeros((2, pairs, 7, 2 * S5_P), F32)
    return dict(toe=toe.astype(BF16),
                wst_re=pair_cols(jnp.real(wst)).astype(BF16), wst_im=pair_cols(jnp.imag(wst)).astype(BF16),
                wi_re=pair_rows(jnp.real(wi)).astype(BF16), wi_im=pair_rows(-jnp.imag(wi)).astype(BF16),
                ak_re=jnp.concatenate([jnp.real(ak), pad4], axis=2), ak_im=jnp.concatenate([jnp.imag(ak), pad4], axis=2),
                pw_re=jnp.real(pw), pw_im=jnp.imag(pw),
                a256_re=jnp.concatenate([jnp.real(a256), pad7], axis=2),
                a256_im=jnp.concatenate([jnp.imag(a256), pad7], axis=2))


def _s5_perm(tt):
    tr = tt // S5_SUB
    r = jnp.arange(tt)
    src = (r % tr) * S5_SUB + r // tr
    return (src[:, None] == jnp.arange(tt)[None, :]).astype(BF16)


def _lane_group(width=128):
    return lax.shift_right_logical(lax.broadcasted_iota(jnp.int32, (1, width), 1), 4)


def _s5_in_kernel(u_ref, p_ref, o_ref, x_ref):
    tr = o_ref.shape[1]
    x_ref[...] = jnp.dot(p_ref[...], u_ref[0], preferred_element_type=F32)
    grp = _lane_group()
    for g in range(S5_G):
        k, gl = divmod(g, 8)
        for half in range(2):
            acc = jnp.zeros((tr, 128), F32)
            for sl in range(8):
                s = half * 8 + sl
                piece = x_ref[s * tr:(s + 1) * tr, 128 * k:128 * (k + 1)]
                shift = ((sl - gl) * S5_H) % 128
                if shift:
                    piece = pltpu.roll(piece, shift, 1)
                acc = jnp.where(grp == sl, piece, acc)
            o_ref[g, :, 128 * half:128 * (half + 1)] = acc.astype(o_ref.dtype)


def _s5_out_kernel(yf_ref, yr_ref, pt_ref, o_ref, z_ref):
    tr = yf_ref.shape[1]
    grp = _lane_group()
    for k in range(4):
        ys = [yf_ref[8 * k + gl].astype(F32) + yr_ref[8 * k + gl].astype(F32) for gl in range(8)]
        for t in range(S5_SUB):
            half, tl = divmod(t, 8)
            acc = jnp.zeros((tr, 128), F32)
            for gl in range(8):
                piece = ys[gl][:, 128 * half:128 * (half + 1)]
                shift = ((gl - tl) * S5_H) % 128
                if shift:
                    piece = pltpu.roll(piece, shift, 1)
                acc = jnp.where(grp == gl, piece, acc)
            z_ref[t * tr:(t + 1) * tr, 128 * k:128 * (k + 1)] = acc.astype(z_ref.dtype)
    o_ref[0] = jnp.dot(pt_ref[...], z_ref[...], preferred_element_type=F32).astype(o_ref.dtype)


def _s5_kernel(u_ref, toe_ref, wre_ref, wim_ref, wire_ref, wiim_ref, akre_ref, akim_ref, pwre_ref, pwim_ref,
               bre_ref, bim_ref, y_ref, ere_ref, eim_ref, gre_ref, gim_ref, *, nb, nblk, rev):
    rows = u_ref.shape[1]
    lanes = 2 * S5_P
    u0 = u_ref[0]
    u1 = u_ref[1]
    ucat = jnp.concatenate([u0, u1], axis=1)
    x_re = jnp.dot(ucat, wre_ref[0, 0], preferred_element_type=F32)
    x_im = jnp.dot(ucat, wim_ref[0, 0], preferred_element_type=F32)
    j = lax.broadcasted_iota(jnp.int32, (rows, 1), 0) & (S5_SUB - 1)
    for i, k in enumerate((1, 2, 4, 8)):
        a_re = akre_ref[0, 0, i:i + 1, :]
        a_im = akim_ref[0, 0, i:i + 1, :]
        sh_re = pltpu.roll(x_re, rows - k if rev else k, 0)
        sh_im = pltpu.roll(x_im, rows - k if rev else k, 0)
        ok = (j <= S5_SUB - 1 - k) if rev else (j >= k)
        x_re, x_im = (x_re + jnp.where(ok, a_re * sh_re - a_im * sh_im, 0.0),
                      x_im + jnp.where(ok, a_re * sh_im + a_im * sh_re, 0.0))
    ok1 = (j <= S5_SUB - 2) if rev else (j >= 1)
    h_re = jnp.where(ok1, pltpu.roll(x_re, rows - 1 if rev else 1, 0), 0.0)
    h_im = jnp.where(ok1, pltpu.roll(x_im, rows - 1 if rev else 1, 0), 0.0)
    last = (j == 0) if rev else (j == S5_SUB - 1)
    nbk = rows // S5_SUB
    ere_ref[...] = jnp.sum(jnp.where(last, x_re, 0.0).reshape(nbk, S5_SUB, lanes), axis=1)
    eim_ref[...] = jnp.sum(jnp.where(last, x_im, 0.0).reshape(nbk, S5_SUB, lanes), axis=1)
    b_re = bre_ref[0, 0, 0:1, :]
    b_im = bim_ref[0, 0, 0:1, :]
    nl = nblk - 1

    def block_step(t, carry):
        blk = jnp.where(t == 0, nl, (nl - t) if rev else (t - 1))
        out = []
        for b in range(nb):
            g_re, g_im = carry[2 * b], carry[2 * b + 1]
            r = b * nblk + blk
            gre_ref[pl.ds(r, 1), :] = g_re
            gim_ref[pl.ds(r, 1), :] = g_im
            out.append(b_re * g_re - b_im * g_im + ere_ref[pl.ds(r, 1), :])
            out.append(b_re * g_im + b_im * g_re + eim_ref[pl.ds(r, 1), :])
        return tuple(out)

    lax.fori_loop(0, nblk, block_step, tuple(jnp.zeros((1, lanes), F32) for _ in range(2 * nb)))

    def per_row(blockwise):
        return jnp.broadcast_to(blockwise[:, None, :], (nbk, S5_SUB, lanes)).reshape(rows, lanes)

    def per_sub(table):
        return jnp.broadcast_to(table[None], (nbk, S5_SUB, lanes)).reshape(rows, lanes)

    g_re = per_row(gre_ref[...])
    g_im = per_row(gim_ref[...])
    q_re = per_sub(pwre_ref[0, 0])
    q_im = per_sub(pwim_ref[0, 0])
    h_re = h_re + q_re * g_re - q_im * g_im
    h_im = h_im + q_re * g_im + q_im * g_re
    y = jnp.dot(h_re.astype(BF16), wire_ref[0, 0], preferred_element_type=F32)
    y += jnp.dot(h_im.astype(BF16), wiim_ref[0, 0], preferred_element_type=F32)
    y_ref[0] = (y[:, 0:256] + jnp.dot(u0, toe_ref[0, 0], preferred_element_type=F32)).astype(y_ref.dtype)
    y_ref[1] = (y[:, 256:512] + jnp.dot(u1, toe_ref[0, 1], preferred_element_type=F32)).astype(y_ref.dtype)


def _s5_scan(z_all, tabs, tt):
    b, nt, _ = z_all.shape
    nblk = nt // CHUNK
    tr = tt // S5_SUB
    rows = b * nt // S5_SUB
    perm = _s5_perm(tt)
    ug = pl.pallas_call(
        _s5_in_kernel,
        grid=(b, nt // tt),
        in_specs=[pl.BlockSpec((1, tt, S5_W), lambda bi, i: (bi, i, C_U // S5_W)),
                  pl.BlockSpec((tt, tt), lambda bi, i: (0, 0))],
        out_specs=pl.BlockSpec((S5_G, tr, 256), lambda bi, i: (0, bi * (nt // tt) + i, 0)),
        out_shape=jax.ShapeDtypeStruct((S5_G, rows, 256), BF16),
        scratch_shapes=[pltpu.VMEM((tt, S5_W), F32)],
        compiler_params=_cparams(("parallel", "parallel")),
        name="s5_in",
    )(z_all, perm)

    def direction(d):
        pair = lambda shape: pl.BlockSpec((1, 1) + shape, lambda p: (d, p, 0, 0))
        return pl.pallas_call(
            functools.partial(_s5_kernel, nb=b, nblk=nblk, rev=bool(d)),
            grid=(S5_G // 2,),
            in_specs=[pl.BlockSpec((2, rows, 256), lambda p: (p, 0, 0)),
                      pl.BlockSpec((1, 2, 256, 256), lambda p: (d, p, 0, 0)),
                      pair((512, 2 * S5_P)), pair((512, 2 * S5_P)), pair((2 * S5_P, 512)), pair((2 * S5_P, 512)),
                      pair((8, 2 * S5_P)), pair((8, 2 * S5_P)), pair((S5_SUB, 2 * S5_P)), pair((S5_SUB, 2 * S5_P)),
                      pair((8, 2 * S5_P)), pair((8, 2 * S5_P))],
            out_specs=pl.BlockSpec((2, rows, 256), lambda p: (p, 0, 0)),
            out_shape=jax.ShapeDtypeStruct((S5_G, rows, 256), BF16),
            scratch_shapes=[pltpu.VMEM((rows // S5_SUB, 2 * S5_P), F32) for _ in range(4)],
            compiler_params=_cparams(("parallel",)),
            name="s5_rev" if d else "s5_fwd",
        )(ug, tabs['toe'], tabs['wst_re'], tabs['wst_im'], tabs['wi_re'], tabs['wi_im'], tabs['ak_re'], tabs['ak_im'],
          tabs['pw_re'], tabs['pw_im'], tabs['a256_re'], tabs['a256_im'])

    yf, yr = direction(0), direction(1)
    return pl.pallas_call(
        _s5_out_kernel,
        grid=(b, nt // tt),
        in_specs=[pl.BlockSpec((S5_G, tr, 256), lambda bi, i: (0, bi * (nt // tt) + i, 0)),
                  pl.BlockSpec((S5_G, tr, 256), lambda bi, i: (0, bi * (nt // tt) + i, 0)),
                  pl.BlockSpec((tt, tt), lambda bi, i: (0, 0))],
        out_specs=pl.BlockSpec((1, tt, S5_W), lambda bi, i: (bi, i, 0)),
        out_shape=jax.ShapeDtypeStruct((b, nt, S5_W), BF16),
        scratch_shapes=[pltpu.VMEM((tt, S5_W), BF16)],
        compiler_params=_cparams(("parallel", "parallel")),
        name="s5_out",
    )(yf, yr, perm.T)


def _ssd_kernel(xbc_ref, prev_ref, next_ref, dt_ref, dtt_ref, cw_ref, cb_ref, bias_ref, biast_ref,
                a_ref, at_ref, dskip_ref, y_ref, h_ref, *, rev, nl):
    step = pl.program_id(1)
    chunk = jnp.where(step == 0, nl, (nl - step) if rev else (step - 1))
    k0 = SSD_NH if rev else 0

    @pl.when(step == 0)
    def _():
        h_ref[...] = jnp.zeros_like(h_ref)

    x = xbc_ref[0].astype(F32)
    is_lat = chunk != nl
    p_ok = jnp.logical_and(is_lat, chunk > 0).astype(F32)
    n_ok = jnp.logical_and(is_lat, chunk < nl - 1).astype(F32)
    prev_row = prev_ref[0, 7:8, :].astype(F32) * p_ok
    next_row = next_ref[0, 0:1, :].astype(F32) * n_ok
    ridx = lax.broadcasted_iota(jnp.int32, (CHUNK, 1), 0)
    xm = jnp.where(ridx == 0, prev_row, pltpu.roll(x, 1, 0))
    xp = jnp.where(ridx == CHUNK - 1, next_row, pltpu.roll(x, CHUNK - 1, 0))
    conv = cw_ref[0:1, :] * xm + cw_ref[1:2, :] * x + cw_ref[2:3, :] * xp + cb_ref[...]
    act = _silu(conv)
    xs = act[:, :SSD_W]
    bm = act[:, SSD_W:SSD_W + SSD_NG * SSD_N].astype(BF16)
    cm = act[:, SSD_W + SSD_NG * SSD_N:].astype(BF16)

    dt = _softplus(dt_ref[0, :, k0:k0 + SSD_NH].astype(F32) + bias_ref[...])
    dtt = _softplus(dtt_ref[0, k0:k0 + SSD_NH, :] + biast_ref[...])
    da = dt * a_ref[...]
    dat = dtt * at_ref[...]
    li = lax.broadcasted_iota(jnp.int32, (CHUNK, CHUNK), 0)
    si = lax.broadcasted_iota(jnp.int32, (CHUNK, CHUNK), 1)
    mask = (si >= li) if rev else (si <= li)
    tri = mask.astype(F32)
    tri_t = ((li >= si) if rev else (li <= si)).astype(F32)
    hp = lax.Precision.HIGHEST
    cum = jnp.dot(tri, da, preferred_element_type=F32, precision=hp)
    cum_t = jnp.dot(dat, tri_t, preferred_element_type=F32, precision=hp)
    total = jnp.sum(da, axis=0, keepdims=True)
    e_cum = jnp.exp(cum)
    w_end = jnp.exp(total - cum) * dt
    e_tot = jnp.exp(total)

    for g in range(SSD_NG):
        c_g = cm[:, g * SSD_N:(g + 1) * SSD_N]
        b_g = bm[:, g * SSD_N:(g + 1) * SSD_N]
        cb = lax.dot_general(c_g, b_g, (((1,), (1,)), ((), ())), preferred_element_type=F32)
        for r in range(SSD_NH // SSD_NG):
            hh = g * (SSD_NH // SSD_NG) + r
            x_h = xs[:, hh * SSD_HD:(hh + 1) * SSD_HD]
            seg = cum[:, hh:hh + 1] - cum_t[hh:hh + 1, :]
            w = (cb * jnp.exp(jnp.where(mask, seg, -jnp.inf))).astype(BF16)
            y = jnp.dot(w, (x_h * dt[:, hh:hh + 1]).astype(BF16), preferred_element_type=F32)
            h_old = h_ref[hh]
            y += jnp.dot(c_g, h_old.astype(BF16), preferred_element_type=F32) * e_cum[:, hh:hh + 1]
            st = lax.dot_general(b_g, (x_h * w_end[:, hh:hh + 1]).astype(BF16), (((0,), (0,)), ((), ())),
                                 preferred_element_type=F32)
            h_ref[hh] = e_tot[:, hh:hh + 1] * h_old + st
            if not rev:
                y += dskip_ref[:, hh * SSD_HD:(hh + 1) * SSD_HD] * x_h
            y_ref[0, :, hh * SSD_HD:(hh + 1) * SSD_HD] = y.astype(y_ref.dtype)


def _ssd_direction(z_all, dtt, conv_w, conv_b, dt_bias, a_log, d_skip, rev):
    b, nt, _ = z_all.shape
    nc = nt // CHUNK
    nl = nc - 1
    k = 1 if rev else 0

    def chunk_of(s):
        return jnp.where(s == 0, nl, (nl - s) if rev else (s - 1))

    a = -jnp.exp(a_log[k].astype(F32))
    bias = dt_bias[k].astype(F32)
    small = lambda shape: pl.BlockSpec(shape, lambda bi, s: (0,) * len(shape))
    return pl.pallas_call(
        functools.partial(_ssd_kernel, rev=rev, nl=nl),
        grid=(b, nc),
        in_specs=[pl.BlockSpec((1, CHUNK, 1024), lambda bi, s: (bi, chunk_of(s), C_XBC // 1024)),
                  pl.BlockSpec((1, 8, 1024), lambda bi, s: (bi, jnp.maximum(chunk_of(s) * 32 - 1, 0), C_XBC // 1024)),
                  pl.BlockSpec((1, 8, 1024), lambda bi, s: (bi, jnp.minimum((chunk_of(s) + 1) * 32, nt // 8 - 1), C_XBC // 1024)),
                  pl.BlockSpec((1, CHUNK, 128), lambda bi, s: (bi, chunk_of(s), C_DT // 128)),
                  pl.BlockSpec((1, 16, CHUNK), lambda bi, s: (bi, 0, chunk_of(s))),
                  small((3, 1024)), small((1, 1024)), small((1, SSD_NH)), small((SSD_NH, 1)),
                  small((1, SSD_NH)), small((SSD_NH, 1)), small((1, SSD_W))],
        out_specs=pl.BlockSpec((1, CHUNK, SSD_W), lambda bi, s: (bi, chunk_of(s), 0)),
        out_shape=jax.ShapeDtypeStruct((b, nt, SSD_W), F32),
        scratch_shapes=[pltpu.VMEM((SSD_NH, SSD_N, SSD_HD), F32)],
        compiler_params=_cparams(("parallel", "arbitrary")),
        name="ssd_rev" if rev else "ssd_fwd",
    )(z_all, z_all, z_all, z_all, dtt, conv_w.astype(F32), conv_b.astype(F32).reshape(1, 1024),
      bias.reshape(1, SSD_NH), bias.reshape(SSD_NH, 1), a.reshape(1, SSD_NH), a.reshape(SSD_NH, 1),
      jnp.repeat(d_skip.astype(F32), SSD_HD).reshape(1, SSD_W))


def _rope(y, cos, sin, width):
    lane = lax.broadcasted_iota(jnp.int32, (1, width), 1)
    first = (lane & (ATT_HD - 1)) < (ATT_HD // 2)
    rot = jnp.where(first, -pltpu.roll(y, width - ATT_HD // 2, 1), pltpu.roll(y, ATT_HD // 2, 1))
    return y * cos + rot * sin


def _qkv_kernel(q_ref, k_ref, v_ref, cos_ref, sin_ref, qn_ref, kn_ref, ones_ref, qt_ref, ko_ref, vt_ref):
    cos = cos_ref[...]
    sin = sin_ref[...]
    tp = cos.shape[0]
    q = q_ref[0].astype(F32)
    ssq = jnp.dot((q * q).astype(BF16), ones_ref[...], preferred_element_type=F32)
    q = q * lax.rsqrt(ssq * (1.0 / ATT_HD) + EPS) * qn_ref[...]
    q = _rope(q, jnp.concatenate([cos] * 4, axis=1), jnp.concatenate([sin] * 4, axis=1), ATT_NH * ATT_HD)
    q = q * (ATT_HD ** -0.5 * LOG2E)
    for g in range(ATT_NKV):
        qt_ref[0, g] = q[:, g * 256:(g + 1) * 256].T.astype(qt_ref.dtype)
    k = k_ref[0].astype(F32)
    ssk = jnp.dot((k * k).astype(BF16), ones_ref[0:128, 0:128], preferred_element_type=F32)
    k = k * lax.rsqrt(ssk * (1.0 / ATT_HD) + EPS) * kn_ref[...]
    k = _rope(k, cos, sin, ATT_NKV * ATT_HD)
    vt = v_ref[0].astype(F32).T
    pad_row = lax.broadcasted_iota(jnp.int32, (VT_ROWS - ATT_HD, tp), 0)
    pad = jnp.where(pad_row == 0, 1.0, 0.0).astype(vt_ref.dtype)
    for g in range(ATT_NKV):
        ko_ref[0, g] = k[:, g * ATT_HD:(g + 1) * ATT_HD].astype(ko_ref.dtype)
        vt_ref[0, g, 0:ATT_HD, :] = vt[g * ATT_HD:(g + 1) * ATT_HD].astype(vt_ref.dtype)
        vt_ref[0, g, ATT_HD:VT_ROWS, :] = pad


def _qkv_prep(z_all, cos, sin, q_norm, k_norm, tp):
    b, nt, _ = z_all.shape
    head = jnp.arange(ATT_NH * ATT_HD) // ATT_HD
    ones = (head[:, None] == head[None, :]).astype(BF16)
    return pl.pallas_call(
        _qkv_kernel,
        grid=(b, nt // tp),
        in_specs=[pl.BlockSpec((1, tp, 512), lambda bi, i: (bi, i, C_Q // 512)),
                  pl.BlockSpec((1, tp, 128), lambda bi, i: (bi, i, C_K // 128)),
                  pl.BlockSpec((1, tp, 128), lambda bi, i: (bi, i, C_V // 128)),
                  pl.BlockSpec((tp, 128), lambda bi, i: (i, 0)),
                  pl.BlockSpec((tp, 128), lambda bi, i: (i, 0)),
                  pl.BlockSpec((1, 512), lambda bi, i: (0, 0)),
                  pl.BlockSpec((1, 128), lambda bi, i: (0, 0)),
                  pl.BlockSpec((512, 512), lambda bi, i: (0, 0))],
        out_specs=[pl.BlockSpec((1, ATT_NKV, 256, tp), lambda bi, i: (bi, 0, 0, i)),
                   pl.BlockSpec((1, ATT_NKV, tp, ATT_HD), lambda bi, i: (bi, 0, i, 0)),
                   pl.BlockSpec((1, ATT_NKV, VT_ROWS, tp), lambda bi, i: (bi, 0, 0, i))],
        out_shape=[jax.ShapeDtypeStruct((b, ATT_NKV, 256, nt), BF16),
                   jax.ShapeDtypeStruct((b, ATT_NKV, nt, ATT_HD), BF16),
                   jax.ShapeDtypeStruct((b, ATT_NKV, VT_ROWS, nt), BF16)],
        compiler_params=_cparams(("parallel", "parallel")),
        name="qkv_prep",
    )(z_all, z_all, z_all, cos, sin, jnp.tile(q_norm.astype(F32), ATT_NH).reshape(1, 512),
      jnp.tile(k_norm.astype(F32), ATT_NKV).reshape(1, 128), ones)


def _rope_tables(n_lat, nt):
    half = ATT_HD // 4
    inv_freq = ROPE_THETA ** (-jnp.arange(half, dtype=F32) / half)
    pos = jnp.arange(n_lat)
    row = (pos // GRID_W).astype(F32)
    col = (pos % GRID_W).astype(F32)
    ang = jnp.concatenate([row[:, None] * inv_freq, col[:, None] * inv_freq], axis=-1)
    cos = jnp.concatenate([jnp.cos(ang), jnp.ones((nt - n_lat, 2 * half), F32)], axis=0)
    sin = jnp.concatenate([jnp.sin(ang), jnp.zeros((nt - n_lat, 2 * half), F32)], axis=0)
    return jnp.tile(cos, (1, 4)), jnp.tile(sin, (1, 4))


def _attn_kernel(qt_ref, k_ref, vt_ref, o_ref, s0_ref, s1_ref, m_ref, acc_ref, *, nk, tk):
    sbuf = (s0_ref, s1_ref)
    m_ref[...] = jnp.full_like(m_ref, -jnp.inf)
    acc_ref[...] = jnp.zeros_like(acc_ref)

    def score(j, h):
        k = k_ref[0, 0, pl.ds(pl.multiple_of(j * tk, tk), tk), :]
        s = jnp.dot(k, qt_ref[0, 0, h * ATT_HD:(h + 1) * ATT_HD, :], preferred_element_type=F32)
        sbuf[h % 2][...] = s
        return jnp.max(s, axis=0, keepdims=True)

    def fold(j, h, tile_max):
        vt = vt_ref[0, 0, :, pl.ds(pl.multiple_of(j * tk, tk), tk)]
        m_old = m_ref[h]
        m_new = jnp.maximum(m_old, tile_max)
        p = jnp.exp2(sbuf[h % 2][...] - m_new).astype(BF16)
        acc_ref[h] = jnp.exp2(m_old - m_new) * acc_ref[h] + jnp.dot(vt, p, preferred_element_type=F32)
        m_ref[h] = m_new

    def key_tile(j, prev_max, first):
        tile_max = score(j, 0)
        if not first:
            fold(j - 1, ATT_GRP - 1, prev_max)
        for h in range(1, ATT_GRP):
            next_max = score(j, h)
            fold(j, h - 1, tile_max)
            tile_max = next_max
        return tile_max

    last_max = key_tile(0, None, True)
    last_max = lax.fori_loop(1, nk, lambda j, pm: key_tile(j, pm, False), last_max)
    fold(nk - 1, ATT_GRP - 1, last_max)

    outs = []
    for h in range(ATT_GRP):
        acc = acc_ref[h]
        outs.append(acc[0:ATT_HD] / acc[ATT_HD:ATT_HD + 1])
    o_ref[0] = jnp.concatenate(outs, axis=0).T.astype(o_ref.dtype)


def _attention(qt, k, vt, nq, tq, tk):
    b = qt.shape[0]
    nkeys = k.shape[2]
    return pl.pallas_call(
        functools.partial(_attn_kernel, nk=nkeys // tk, tk=tk),
        grid=(b, ATT_NKV, nq // tq),
        in_specs=[pl.BlockSpec((1, 1, 256, tq), lambda bi, g, i: (bi, g, 0, i)),
                  pl.BlockSpec((1, 1, nkeys, ATT_HD), lambda bi, g, i: (bi, g, 0, 0)),
                  pl.BlockSpec((1, 1, VT_ROWS, nkeys), lambda bi, g, i: (bi, g, 0, 0))],
        out_specs=pl.BlockSpec((1, tq, 256), lambda bi, g, i: (bi, i, g)),
        out_shape=jax.ShapeDtypeStruct((b, nq, ATT_NH * ATT_HD), BF16),
        scratch_shapes=[pltpu.VMEM((tk, tq), F32), pltpu.VMEM((tk, tq), F32),
                        pltpu.VMEM((ATT_GRP, 1, tq), F32), pltpu.VMEM((ATT_GRP, VT_ROWS, tq), F32)],
        compiler_params=_cparams(("parallel", "parallel", "parallel")),
        name="attention",
    )(qt, k, vt)


def _merge_kernel(x_ref, mg_ref, u_ref, ug_ref, z_ref, ag_ref, y5_ref, ysf_ref, ysr_ref, yc_ref,
                  mod_ref, s5d_ref, nrm_ref, pn_ref, glu_ref, wa_ref, wb_ref, wc_ref, wo_ref, o_ref, *, n_lat, tm):
    i = pl.program_id(1)
    f = lambda r: r[0].astype(F32)
    u = f(u_ref)
    ya = jax.nn.gelu(f(y5_ref) + s5d_ref[...] * u, approximate=True)
    ya = ya * jax.nn.sigmoid(jnp.dot(ya.astype(BF16), glu_ref[...], preferred_element_type=F32))
    ya = ya * _silu(f(ug_ref))
    yb = (f(ysf_ref) + f(ysr_ref)) * _silu(f(z_ref))
    yb = yb * lax.rsqrt(jnp.mean(yb * yb, axis=-1, keepdims=True) + EPS) * nrm_ref[...]
    yc = f(yc_ref) * _silu(f(ag_ref))
    gates = jax.nn.sigmoid(f(mg_ref))
    mix = gates[:, 0:D] * jnp.dot(ya.astype(BF16), wa_ref[...], preferred_element_type=F32)
    mix += gates[:, D:2 * D] * jnp.dot(yb.astype(BF16), wb_ref[...], preferred_element_type=F32)
    mix += gates[:, 2 * D:3 * D] * jnp.dot(yc.astype(BF16), wc_ref[...], preferred_element_type=F32)
    out = jnp.dot(mix.astype(BF16), wo_ref[...], preferred_element_type=F32)
    out = out * lax.rsqrt(jnp.mean(out * out, axis=-1, keepdims=True) + EPS) * pn_ref[...]
    m = mod_ref[0]
    row = i * tm + lax.broadcasted_iota(jnp.int32, (tm, 1), 0)
    gate = jnp.where(row >= n_lat, m[5:6], m[2:3])
    o_ref[0] = x_ref[0] + gate * out


def _merge(x_all, z_all, y5, ysf, ysr, yc, modv, s5_d, ssd_norm, post_norm, glu, wa, wb, wc, wo,
           n_lat, n_rows, tm):
    b = x_all.shape[0]
    tok = lambda w, c: pl.BlockSpec((1, tm, w), lambda bi, i: (bi, i, c))
    full = lambda shape: pl.BlockSpec(shape, lambda bi, i: (0,) * len(shape))
    return pl.pallas_call(
        functools.partial(_merge_kernel, n_lat=n_lat, tm=tm),
        grid=(b, n_rows // tm),
        in_specs=[tok(D, 0), tok(3 * D, 0), tok(512, C_U // 512), tok(512, C_UG // 512), tok(512, C_Z // 512),
                  tok(512, C_AG // 512), tok(512, 0), tok(512, 0), tok(512, 0), tok(512, 0),
                  pl.BlockSpec((1, 8, D), lambda bi, i: (bi, 0, 0)),
                  full((1, 512)), full((1, 512)), full((1, D)),
                  full((512, 512)), full((512, D)), full((512, D)), full((512, D)), full((D, D))],
        out_specs=tok(D, 0),
        out_shape=jax.ShapeDtypeStruct((b, n_rows, D), F32),
        compiler_params=_cparams(("parallel", "parallel")),
        name="merge",
    )(x_all, z_all, z_all, z_all, z_all, z_all, y5, ysf, ysr, yc, modv,
      s5_d.astype(F32).reshape(1, 512), ssd_norm.astype(F32).reshape(1, 512), post_norm.astype(F32).reshape(1, D),
      glu.astype(BF16), wa.astype(BF16), wb.astype(BF16), wc.astype(BF16), wo.astype(BF16))


def _largest_divisor(n, cands):
    for c in cands:
        if n % c == 0:
            return c
    raise ValueError(f"no tile in {cands} divides {n}")


def _layer(x_all, mod_l, cos, sin, n_lat, with_ctx, pre_norm, post_norm, w_in, s5_lam_re, s5_lam_im, s5_log_dt,
           s5_b_re, s5_b_im, s5_c_re, s5_c_im, s5_d, s5_glu, ssd_conv_w, ssd_conv_b, ssd_dt_bias, ssd_a_log, ssd_d,
           ssd_norm, att_q_norm, att_k_norm, w_branch_a, w_branch_b, w_branch_c, w_out):
    bsz, nt, _ = x_all.shape
    n_ctx = nt - n_lat
    tm_proj = CHUNK * _largest_divisor(nt // CHUNK, (5, 4, 3, 2, 1))
    tq = _largest_divisor(n_lat, (512, 256))
    tk = CHUNK * _largest_divisor(nt // CHUNK, (5, 4, 3, 2, 1))
    ml = mod_l[:bsz].reshape(bsz, 3, D)
    mc = jnp.broadcast_to(mod_l[bsz].reshape(1, 3, D), (bsz, 3, D))
    modv = jnp.concatenate([ml, mc, jnp.zeros((bsz, 2, D), F32)], axis=1)
    z_all = _in_proj(x_all, modv, pre_norm.astype(F32), _pack_w_in(w_in), n_lat, tm_proj)

    tabs = _s5_tables(s5_lam_re, s5_lam_im, s5_log_dt, s5_b_re, s5_b_im, s5_c_re, s5_c_im)
    y5 = _s5_scan(z_all, tabs, tm_proj)

    dtt = z_all[:, :, C_DT:C_DT + 2 * SSD_NH].astype(F32).transpose(0, 2, 1)
    ssd_args = (z_all, dtt, ssd_conv_w, ssd_conv_b, ssd_dt_bias, ssd_a_log, ssd_d)
    ysf = _ssd_direction(*ssd_args, rev=False)
    ysr = _ssd_direction(*ssd_args, rev=True)

    qt, kk, vt = _qkv_prep(z_all, cos, sin, att_q_norm, att_k_norm, CHUNK)
    yc = _attention(qt, kk, vt, n_lat, tq, tk)
    if with_ctx:
        yc_ctx = _attention(qt[..., n_lat:], kk[:, :, n_lat:], vt[..., n_lat:], n_ctx, n_ctx, n_ctx)
        yc = jnp.concatenate([yc, yc_ctx], axis=1)
    n_rows = nt if with_ctx else n_lat
    return _merge(x_all, z_all, y5, ysf, ysr, yc, modv, s5_d, ssd_norm, post_norm,
                  s5_glu, w_branch_a, w_branch_b, w_branch_c, w_out, n_lat, n_rows, CHUNK)


def kernel(x, c, ctx, c_ctx, mod_w, mod_b, pre_norm, post_norm, w_in, s5_lam_re, s5_lam_im, s5_log_dt, s5_b_re, s5_b_im, s5_c_re, s5_c_im, s5_d, s5_glu, ssd_conv_w, ssd_conv_b, ssd_dt_bias, ssd_a_log, ssd_d, ssd_norm, att_q_norm, att_k_norm, w_branch_a, w_branch_b, w_branch_c, w_out):
    bsz, n_lat, _ = x.shape
    n_ctx = ctx.shape[1]
    depth = mod_w.shape[0]
    nt = n_lat + n_ctx
    assert n_lat % CHUNK == 0 and n_ctx == CHUNK and n_lat % GRID_W == 0 and bsz + 1 <= 8
    c8 = jnp.concatenate([c.astype(F32), c_ctx.astype(F32)[None], jnp.zeros((8 - bsz - 1, D), F32)], axis=0)
    mod = _modulation(c8, mod_w.astype(F32), mod_b.astype(F32))
    cos, sin = _rope_tables(n_lat, nt)
    x_all = jnp.concatenate([x.astype(F32), ctx.astype(F32)], axis=1)
    per_layer = (pre_norm, post_norm, w_in, s5_lam_re, s5_lam_im, s5_log_dt, s5_b_re, s5_b_im, s5_c_re, s5_c_im,
                 s5_d, s5_glu, ssd_conv_w, ssd_conv_b, ssd_dt_bias, ssd_a_log, ssd_d, ssd_norm, att_q_norm,
                 att_k_norm, w_branch_a, w_branch_b, w_branch_c, w_out)
    for l in range(depth):
        x_all = _layer(x_all, mod[l], cos, sin, n_lat, l < depth - 1, *[p[l] for p in per_layer])
    return x_all.astype(x.dtype)
```

```python
import functools

import jax
import jax.numpy as jnp
from jax import lax
from jax.experimental import pallas as pl
from jax.experimental.pallas import tpu as pltpu

F32 = jnp.float32
BF16 = jnp.bfloat16
F8 = jnp.float8_e4m3fn

D = 1024
EPS = 1e-6
GRID_W = 64
S5_G, S5_H, S5_P = 32, 16, 64
S5_W = S5_G * S5_H
S5_SUB = 16
SSD_W, SSD_HD, SSD_NH, SSD_NG, SSD_N = 512, 64, 8, 2, 128
CHUNK = 256
ATT_HD, ATT_NH, ATT_NKV = 64, 8, 2
ATT_GRP = ATT_NH // ATT_NKV
ROPE_THETA = 10000.0
LOG2E = 1.4426950408889634
QK_SCALE = (ATT_HD ** -0.5 * LOG2E) ** 0.5
VT_ROWS = 80

C_MG, C_U, C_UG, C_XBC, C_Z, C_Q, C_AG, C_K, C_V, C_DT = 0, 3072, 3584, 4096, 5120, 5632, 6144, 6656, 6784, 6912
ZW = 7168
PROJ_TN = 1024
VMEM_LIMIT = 56 * 1024 * 1024


def _cparams(sem):
    return pltpu.CompilerParams(dimension_semantics=sem, vmem_limit_bytes=VMEM_LIMIT)


def _silu(x):
    return x * jax.nn.sigmoid(x)


def _softplus(x):
    return jnp.maximum(x, 0.0) + jnp.log1p(jnp.exp(-jnp.abs(x)))


def _mod_kernel(c_ref, w_ref, b_ref, o_ref):
    c = c_ref[...]
    o_ref[0] = jnp.dot(_silu(c), w_ref[0], preferred_element_type=F32,
                       precision=lax.Precision.HIGHEST) + b_ref[0]


def _modulation(c8, mod_w, mod_b):
    nl = mod_w.shape[0]
    return pl.pallas_call(
        _mod_kernel,
        grid=(nl, 3),
        in_specs=[pl.BlockSpec((8, D), lambda l, j: (0, 0)),
                  pl.BlockSpec((1, D, D), lambda l, j: (l, 0, j)),
                  pl.BlockSpec((1, 1, D), lambda l, j: (l, 0, j))],
        out_specs=pl.BlockSpec((1, 8, D), lambda l, j: (l, 0, j)),
        out_shape=jax.ShapeDtypeStruct((nl, 8, 3 * D), F32),
        compiler_params=_cparams(("parallel", "parallel")),
        name="modulation",
    )(c8, mod_w, mod_b.reshape(nl, 1, 3 * D))


def _proj_kernel(x_ref, mod_ref, pn_ref, w_ref, o_ref, h_ref, *, n_lat, tm):
    i = pl.program_id(1)

    @pl.when(pl.program_id(2) == 0)
    def _():
        x = x_ref[0]
        y = x * lax.rsqrt(jnp.mean(x * x, axis=-1, keepdims=True) + EPS) * pn_ref[...]
        m = mod_ref[0]
        h_l = y * (1.0 + m[1:2]) + m[0:1]
        h_c = y * (1.0 + m[4:5]) + m[3:4]
        row = i * tm + lax.broadcasted_iota(jnp.int32, (tm, 1), 0)
        h_ref[...] = jnp.where(row >= n_lat, h_c, h_l).astype(BF16)

    o_ref[0] = jnp.dot(h_ref[...], w_ref[...], preferred_element_type=F32).astype(o_ref.dtype)


def _in_proj(x_all, modv, pre_norm, w, n_lat, tm):
    b, nt, _ = x_all.shape
    return pl.pallas_call(
        functools.partial(_proj_kernel, n_lat=n_lat, tm=tm),
        grid=(b, nt // tm, ZW // PROJ_TN),
        in_specs=[pl.BlockSpec((1, tm, D), lambda bi, i, j: (bi, i, 0)),
                  pl.BlockSpec((1, 8, D), lambda bi, i, j: (bi, 0, 0)),
                  pl.BlockSpec((1, D), lambda bi, i, j: (0, 0)),
                  pl.BlockSpec((D, PROJ_TN), lambda bi, i, j: (0, j))],
        out_specs=pl.BlockSpec((1, tm, PROJ_TN), lambda bi, i, j: (bi, i, j)),
        out_shape=jax.ShapeDtypeStruct((b, nt, ZW), BF16),
        scratch_shapes=[pltpu.VMEM((tm, D), BF16)],
        compiler_params=_cparams(("parallel", "parallel", "arbitrary")),
        name="in_proj",
    )(x_all, modv, pre_norm.reshape(1, D), w)


def _pack_w_in(w):
    u, ug, xbc, z, dt, q, k, v, ag, mg = jnp.split(
        w, [512, 1024, 2048, 2560, 2576, 3088, 3216, 3344, 3856], axis=-1)
    dt = jnp.pad(dt, ((0, 0), (0, ZW - C_DT - dt.shape[-1])))
    return jnp.concatenate([mg, u, ug, xbc, z, q, ag, k, v, dt], axis=-1).astype(BF16)


def _s5_tables(lam_re, lam_im, log_dt, b_re, b_im, c_re, c_im):
    lam = lax.complex(lam_re.astype(F32), lam_im.astype(F32))
    ladt = lam * jnp.exp(log_dt.astype(F32))[..., None]
    b_bar = ((jnp.exp(ladt) - 1.0) / lam)[..., None] * lax.complex(b_re.astype(F32), b_im.astype(F32))
    cm = lax.complex(c_re.astype(F32), c_im.astype(F32))
    n = jnp.arange(S5_SUB, dtype=F32)
    fwd = jnp.array([1.0, 0.0], F32)

    def powers(e_fwd, e_rev):
        e = e_fwd[:, None] * fwd + e_rev[:, None] * (1.0 - fwd)
        return jnp.exp(e[:, :, None, None] * ladt[None])

    kd = jnp.real(jnp.einsum('kghp,dkgp,kgpj->dkghj', cm, powers(n, n), b_bar))
    s_idx = jnp.arange(S5_SUB)[:, None]
    t_idx = jnp.arange(S5_SUB)[None, :]
    lag = jnp.stack([t_idx - s_idx, s_idx - t_idx])
    kdt = kd.transpose(1, 0, 2, 3, 4)
    toe = jnp.stack([kdt[k][jnp.clip(lag[k], 0, S5_SUB - 1)] for k in range(2)])
    toe = jnp.where((lag >= 0)[:, :, :, None, None, None], toe, 0.0)
    toe = toe.transpose(0, 3, 1, 5, 2, 4).reshape(2, S5_G, 256, 256)
    wst = jnp.einsum('skgp,kgpj->kgsjp', powers(S5_SUB - 1.0 - n, n), b_bar).reshape(2, S5_G, 256, S5_P)
    wi = jnp.einsum('kghp,tkgp->kgpth', cm, powers(n + 1.0, S5_SUB - n)).reshape(2, S5_G, S5_P, 256)
    pairs = S5_G // 2

    def pair_cols(w):
        w = w.reshape(2, pairs, 2, 256, S5_P)
        z = jnp.zeros_like(w[:, :, 0])
        return jnp.concatenate([jnp.concatenate([w[:, :, 0], z], axis=-1),
                                jnp.concatenate([z, w[:, :, 1]], axis=-1)], axis=-2)

    def pair_rows(w):
        w = w.reshape(2, pairs, 2, S5_P, 256)
        z = jnp.zeros_like(w[:, :, 0])
        return jnp.concatenate([jnp.concatenate([w[:, :, 0], z], axis=-1),
                                jnp.concatenate([z, w[:, :, 1]], axis=-1)], axis=-2)

    def pair_lanes(p):
        return p.transpose(1, 2, 0, 3).reshape(2, pairs, 2, p.shape[0], S5_P).transpose(0, 1, 3, 2, 4).reshape(
            2, pairs, p.shape[0], 2 * S5_P)

    steps = jnp.array([1.0, 2.0, 4.0, 8.0], F32) * S5_SUB
    ak = pair_lanes(powers(steps, steps))
    pw = pair_lanes(powers(S5_SUB * n, S5_SUB * (S5_SUB - 1.0 - n)))
    a256 = pair_lanes(powers(jnp.array([float(CHUNK)], F32), jnp.array([float(CHUNK)], F32)))
    pad4 = jnp.zeros((2, pairs, 4, 2 * S5_P), F32)
    pad7 = jnp.zeros((2, pairs, 7, 2 * S5_P), F32)
    return dict(toe=toe.astype(BF16),
                wst_re=pair_cols(jnp.real(wst)).astype(BF16), wst_im=pair_cols(jnp.imag(wst)).astype(BF16),
                wi_re=pair_rows(jnp.real(wi)).astype(BF16), wi_im=pair_rows(-jnp.imag(wi)).astype(BF16),
                ak_re=jnp.concatenate([jnp.real(ak), pad4], axis=2), ak_im=jnp.concatenate([jnp.imag(ak), pad4], axis=2),
                pw_re=jnp.real(pw), pw_im=jnp.imag(pw),
                a256_re=jnp.concatenate([jnp.real(a256), pad7], axis=2),
                a256_im=jnp.concatenate([jnp.imag(a256), pad7], axis=2))


def _s5_perm(tt):
    tr = tt // S5_SUB
    r = jnp.arange(tt)
    src = (r % tr) * S5_SUB + r // tr
    return (src[:, None] == jnp.arange(tt)[None, :]).astype(BF16)


def _lane_group(width=128):
    return lax.shift_right_logical(lax.broadcasted_iota(jnp.int32, (1, width), 1), 4)


def _s5_in_kernel(u_ref, p_ref, o_ref, x_ref):
    tr = o_ref.shape[1]
    x_ref[...] = jnp.dot(p_ref[...], u_ref[0], preferred_element_type=F32)
    grp = _lane_group()
    for g in range(S5_G):
        k, gl = divmod(g, 8)
        for half in range(2):
            acc = jnp.zeros((tr, 128), F32)
            for sl in range(8):
                s = half * 8 + sl
                piece = x_ref[s * tr:(s + 1) * tr, 128 * k:128 * (k + 1)]
                shift = ((sl - gl) * S5_H) % 128
                if shift:
                    piece = pltpu.roll(piece, shift, 1)
                acc = jnp.where(grp == sl, piece, acc)
            o_ref[g, :, 128 * half:128 * (half + 1)] = acc.astype(o_ref.dtype)


def _s5_out_kernel(yf_ref, yr_ref, pt_ref, o_ref, z_ref):
    tr = yf_ref.shape[1]
    grp = _lane_group()
    for k in range(4):
        ys = [yf_ref[8 * k + gl].astype(F32) + yr_ref[8 * k + gl].astype(F32) for gl in range(8)]
        for t in range(S5_SUB):
            half, tl = divmod(t, 8)
            acc = jnp.zeros((tr, 128), F32)
            for gl in range(8):
                piece = ys[gl][:, 128 * half:128 * (half + 1)]
                shift = ((gl - tl) * S5_H) % 128
                if shift:
                    piece = pltpu.roll(piece, shift, 1)
                acc = jnp.where(grp == gl, piece, acc)
            z_ref[t * tr:(t + 1) * tr, 128 * k:128 * (k + 1)] = acc.astype(z_ref.dtype)
    o_ref[0] = jnp.dot(pt_ref[...], z_ref[...], preferred_element_type=F32).astype(o_ref.dtype)


def _s5_kernel(u_ref, toe_ref, wre_ref, wim_ref, wire_ref, wiim_ref, akre_ref, akim_ref, pwre_ref, pwim_ref,
               bre_ref, bim_ref, y_ref, ere_ref, eim_ref, gre_ref, gim_ref, *, nb, nblk, rev):
    rows = u_ref.shape[1]
    lanes = 2 * S5_P
    u0 = u_ref[0]
    u1 = u_ref[1]
    ucat = jnp.concatenate([u0, u1], axis=1)
    x_re = jnp.dot(ucat, wre_ref[0, 0], preferred_element_type=F32)
    x_im = jnp.dot(ucat, wim_ref[0, 0], preferred_element_type=F32)
    j = lax.broadcasted_iota(jnp.int32, (rows, 1), 0) & (S5_SUB - 1)
    for i, k in enumerate((1, 2, 4, 8)):
        a_re = akre_ref[0, 0, i:i + 1, :]
        a_im = akim_ref[0, 0, i:i + 1, :]
        sh_re = pltpu.roll(x_re, rows - k if rev else k, 0)
        sh_im = pltpu.roll(x_im, rows - k if rev else k, 0)
        ok = (j <= S5_SUB - 1 - k) if rev else (j >= k)
        x_re, x_im = (x_re + jnp.where(ok, a_re * sh_re - a_im * sh_im, 0.0),
                      x_im + jnp.where(ok, a_re * sh_im + a_im * sh_re, 0.0))
    ok1 = (j <= S5_SUB - 2) if rev else (j >= 1)
    h_re = jnp.where(ok1, pltpu.roll(x_re, rows - 1 if rev else 1, 0), 0.0)
    h_im = jnp.where(ok1, pltpu.roll(x_im, rows - 1 if rev else 1, 0), 0.0)
    last = (j == 0) if rev else (j == S5_SUB - 1)
    nbk = rows // S5_SUB
    ere_ref[...] = jnp.sum(jnp.where(last, x_re, 0.0).reshape(nbk, S5_SUB, lanes), axis=1)
    eim_ref[...] = jnp.sum(jnp.where(last, x_im, 0.0).reshape(nbk, S5_SUB, lanes), axis=1)
    b_re = bre_ref[0, 0, 0:1, :]
    b_im = bim_ref[0, 0, 0:1, :]
    nl = nblk - 1

    def block_step(t, carry):
        blk = jnp.where(t == 0, nl, (nl - t) if rev else (t - 1))
        out = []
        for b in range(nb):
            g_re, g_im = carry[2 * b], carry[2 * b + 1]
            r = b * nblk + blk
            gre_ref[pl.ds(r, 1), :] = g_re
            gim_ref[pl.ds(r, 1), :] = g_im
            out.append(b_re * g_re - b_im * g_im + ere_ref[pl.ds(r, 1), :])
            out.append(b_re * g_im + b_im * g_re + eim_ref[pl.ds(r, 1), :])
        return tuple(out)

    lax.fori_loop(0, nblk, block_step, tuple(jnp.zeros((1, lanes), F32) for _ in range(2 * nb)))

    def per_row(blockwise):
        return jnp.broadcast_to(blockwise[:, None, :], (nbk, S5_SUB, lanes)).reshape(rows, lanes)

    def per_sub(table):
        return jnp.broadcast_to(table[None], (nbk, S5_SUB, lanes)).reshape(rows, lanes)

    g_re = per_row(gre_ref[...])
    g_im = per_row(gim_ref[...])
    q_re = per_sub(pwre_ref[0, 0])
    q_im = per_sub(pwim_ref[0, 0])
    h_re = h_re + q_re * g_re - q_im * g_im
    h_im = h_im + q_re * g_im + q_im * g_re
    y = jnp.dot(h_re.astype(BF16), wire_ref[0, 0], preferred_element_type=F32)
    y += jnp.dot(h_im.astype(BF16), wiim_ref[0, 0], preferred_element_type=F32)
    y_ref[0] = (y[:, 0:256] + jnp.dot(u0, toe_ref[0, 0], preferred_element_type=F32)).astype(y_ref.dtype)
    y_ref[1] = (y[:, 256:512] + jnp.dot(u1, toe_ref[0, 1], preferred_element_type=F32)).astype(y_ref.dtype)


def _s5_scan(z_all, tabs, tt):
    b, nt, _ = z_all.shape
    nblk = nt // CHUNK
    tr = tt // S5_SUB
    rows = b * nt // S5_SUB
    perm = _s5_perm(tt)
    ug = pl.pallas_call(
        _s5_in_kernel,
        grid=(b, nt // tt),
        in_specs=[pl.BlockSpec((1, tt, S5_W), lambda bi, i: (bi, i, C_U // S5_W)),
                  pl.BlockSpec((tt, tt), lambda bi, i: (0, 0))],
        out_specs=pl.BlockSpec((S5_G, tr, 256), lambda bi, i: (0, bi * (nt // tt) + i, 0)),
        out_shape=jax.ShapeDtypeStruct((S5_G, rows, 256), BF16),
        scratch_shapes=[pltpu.VMEM((tt, S5_W), F32)],
        compiler_params=_cparams(("parallel", "parallel")),
        name="s5_in",
    )(z_all, perm)

    def direction(d):
        pair = lambda shape: pl.BlockSpec((1, 1) + shape, lambda p: (d, p, 0, 0))
        return pl.pallas_call(
            functools.partial(_s5_kernel, nb=b, nblk=nblk, rev=bool(d)),
            grid=(S5_G // 2,),
            in_specs=[pl.BlockSpec((2, rows, 256), lambda p: (p, 0, 0)),
                      pl.BlockSpec((1, 2, 256, 256), lambda p: (d, p, 0, 0)),
                      pair((512, 2 * S5_P)), pair((512, 2 * S5_P)), pair((2 * S5_P, 512)), pair((2 * S5_P, 512)),
                      pair((8, 2 * S5_P)), pair((8, 2 * S5_P)), pair((S5_SUB, 2 * S5_P)), pair((S5_SUB, 2 * S5_P)),
                      pair((8, 2 * S5_P)), pair((8, 2 * S5_P))],
            out_specs=pl.BlockSpec((2, rows, 256), lambda p: (p, 0, 0)),
            out_shape=jax.ShapeDtypeStruct((S5_G, rows, 256), BF16),
            scratch_shapes=[pltpu.VMEM((rows // S5_SUB, 2 * S5_P), F32) for _ in range(4)],
            compiler_params=_cparams(("parallel",)),
            name="s5_rev" if d else "s5_fwd",
        )(ug, tabs['toe'], tabs['wst_re'], tabs['wst_im'], tabs['wi_re'], tabs['wi_im'], tabs['ak_re'], tabs['ak_im'],
          tabs['pw_re'], tabs['pw_im'], tabs['a256_re'], tabs['a256_im'])

    yf, yr = direction(0), direction(1)
    return pl.pallas_call(
        _s5_out_kernel,
        grid=(b, nt // tt),
        in_specs=[pl.BlockSpec((S5_G, tr, 256), lambda bi, i: (0, bi * (nt // tt) + i, 0)),
                  pl.BlockSpec((S5_G, tr, 256), lambda bi, i: (0, bi * (nt // tt) + i, 0)),
                  pl.BlockSpec((tt, tt), lambda bi, i: (0, 0))],
        out_specs=pl.BlockSpec((1, tt, S5_W), lambda bi, i: (bi, i, 0)),
        out_shape=jax.ShapeDtypeStruct((b, nt, S5_W), BF16),
        scratch_shapes=[pltpu.VMEM((tt, S5_W), BF16)],
        compiler_params=_cparams(("parallel", "parallel")),
        name="s5_out",
    )(yf, yr, perm.T)


def _ssd_kernel(xbc_ref, prev_ref, next_ref, dt_ref, dtt_ref, cw_ref, cb_ref, bias_ref, biast_ref,
                a_ref, at_ref, dskip_ref, y_ref, h_ref, *, rev, nl):
    step = pl.program_id(1)
    chunk = jnp.where(step == 0, nl, (nl - step) if rev else (step - 1))
    k0 = SSD_NH if rev else 0

    @pl.when(step == 0)
    def _():
        h_ref[...] = jnp.zeros_like(h_ref)

    x = xbc_ref[0].astype(F32)
    is_lat = chunk != nl
    p_ok = jnp.logical_and(is_lat, chunk > 0).astype(F32)
    n_ok = jnp.logical_and(is_lat, chunk < nl - 1).astype(F32)
    prev_row = prev_ref[0, 7:8, :].astype(F32) * p_ok
    next_row = next_ref[0, 0:1, :].astype(F32) * n_ok
    ridx = lax.broadcasted_iota(jnp.int32, (CHUNK, 1), 0)
    xm = jnp.where(ridx == 0, prev_row, pltpu.roll(x, 1, 0))
    xp = jnp.where(ridx == CHUNK - 1, next_row, pltpu.roll(x, CHUNK - 1, 0))
    conv = cw_ref[0:1, :] * xm + cw_ref[1:2, :] * x + cw_ref[2:3, :] * xp + cb_ref[...]
    act = _silu(conv)
    xs = act[:, :SSD_W]
    bm = act[:, SSD_W:SSD_W + SSD_NG * SSD_N].astype(BF16)
    cm = act[:, SSD_W + SSD_NG * SSD_N:].astype(BF16)

    dt = _softplus(dt_ref[0, :, k0:k0 + SSD_NH].astype(F32) + bias_ref[...])
    dtt = _softplus(dtt_ref[0, k0:k0 + SSD_NH, :] + biast_ref[...])
    da = dt * a_ref[...]
    dat = dtt * at_ref[...]
    li = lax.broadcasted_iota(jnp.int32, (CHUNK, CHUNK), 0)
    si = lax.broadcasted_iota(jnp.int32, (CHUNK, CHUNK), 1)
    mask = (si >= li) if rev else (si <= li)
    tri = mask.astype(F32)
    tri_t = ((li >= si) if rev else (li <= si)).astype(F32)
    hp = lax.Precision.HIGHEST
    cum = jnp.dot(tri, da, preferred_element_type=F32, precision=hp)
    cum_t = jnp.dot(dat, tri_t, preferred_element_type=F32, precision=hp)
    total = jnp.sum(da, axis=0, keepdims=True)
    e_cum = jnp.exp(cum)
    w_end = jnp.exp(total - cum) * dt
    e_tot = jnp.exp(total)

    for g in range(SSD_NG):
        c_g = cm[:, g * SSD_N:(g + 1) * SSD_N]
        b_g = bm[:, g * SSD_N:(g + 1) * SSD_N]
        cb = lax.dot_general(c_g, b_g, (((1,), (1,)), ((), ())), preferred_element_type=F32)
        for r in range(SSD_NH // SSD_NG):
            hh = g * (SSD_NH // SSD_NG) + r
            x_h = xs[:, hh * SSD_HD:(hh + 1) * SSD_HD]
            seg = cum[:, hh:hh + 1] - cum_t[hh:hh + 1, :]
            w = (cb * jnp.exp(jnp.where(mask, seg, -jnp.inf))).astype(BF16)
            y = jnp.dot(w, (x_h * dt[:, hh:hh + 1]).astype(BF16), preferred_element_type=F32)
            h_old = h_ref[hh]
            y += jnp.dot(c_g, h_old.astype(BF16), preferred_element_type=F32) * e_cum[:, hh:hh + 1]
            st = lax.dot_general(b_g, (x_h * w_end[:, hh:hh + 1]).astype(BF16), (((0,), (0,)), ((), ())),
                                 preferred_element_type=F32)
            h_ref[hh] = e_tot[:, hh:hh + 1] * h_old + st
            if not rev:
                y += dskip_ref[:, hh * SSD_HD:(hh + 1) * SSD_HD] * x_h
            y_ref[0, :, hh * SSD_HD:(hh + 1) * SSD_HD] = y.astype(y_ref.dtype)


def _ssd_direction(z_all, dtt, conv_w, conv_b, dt_bias, a_log, d_skip, rev):
    b, nt, _ = z_all.shape
    nc = nt // CHUNK
    nl = nc - 1
    k = 1 if rev else 0

    def chunk_of(s):
        return jnp.where(s == 0, nl, (nl - s) if rev else (s - 1))

    a = -jnp.exp(a_log[k].astype(F32))
    bias = dt_bias[k].astype(F32)
    small = lambda shape: pl.BlockSpec(shape, lambda bi, s: (0,) * len(shape))
    return pl.pallas_call(
        functools.partial(_ssd_kernel, rev=rev, nl=nl),
        grid=(b, nc),
        in_specs=[pl.BlockSpec((1, CHUNK, 1024), lambda bi, s: (bi, chunk_of(s), C_XBC // 1024)),
                  pl.BlockSpec((1, 8, 1024), lambda bi, s: (bi, jnp.maximum(chunk_of(s) * 32 - 1, 0), C_XBC // 1024)),
                  pl.BlockSpec((1, 8, 1024), lambda bi, s: (bi, jnp.minimum((chunk_of(s) + 1) * 32, nt // 8 - 1), C_XBC // 1024)),
                  pl.BlockSpec((1, CHUNK, 128), lambda bi, s: (bi, chunk_of(s), C_DT // 128)),
                  pl.BlockSpec((1, 16, CHUNK), lambda bi, s: (bi, 0, chunk_of(s))),
                  small((3, 1024)), small((1, 1024)), small((1, SSD_NH)), small((SSD_NH, 1)),
                  small((1, SSD_NH)), small((SSD_NH, 1)), small((1, SSD_W))],
        out_specs=pl.BlockSpec((1, CHUNK, SSD_W), lambda bi, s: (bi, chunk_of(s), 0)),
        out_shape=jax.ShapeDtypeStruct((b, nt, SSD_W), F32),
        scratch_shapes=[pltpu.VMEM((SSD_NH, SSD_N, SSD_HD), F32)],
        compiler_params=_cparams(("parallel", "arbitrary")),
        name="ssd_rev" if rev else "ssd_fwd",
    )(z_all, z_all, z_all, z_all, dtt, conv_w.astype(F32), conv_b.astype(F32).reshape(1, 1024),
      bias.reshape(1, SSD_NH), bias.reshape(SSD_NH, 1), a.reshape(1, SSD_NH), a.reshape(SSD_NH, 1),
      jnp.repeat(d_skip.astype(F32), SSD_HD).reshape(1, SSD_W))


def _rope(y, cos, sin, width):
    lane = lax.broadcasted_iota(jnp.int32, (1, width), 1)
    first = (lane & (ATT_HD - 1)) < (ATT_HD // 2)
    rot = jnp.where(first, -pltpu.roll(y, width - ATT_HD // 2, 1), pltpu.roll(y, ATT_HD // 2, 1))
    return y * cos + rot * sin


def _qkv_kernel(q_ref, k_ref, v_ref, cos_ref, sin_ref, qn_ref, kn_ref, ones_ref, qt_ref, ko_ref, vt_ref):
    cos = cos_ref[...]
    sin = sin_ref[...]
    tp = cos.shape[0]
    q = q_ref[0].astype(F32)
    ssq = jnp.dot((q * q).astype(BF16), ones_ref[...], preferred_element_type=F32)
    q = q * lax.rsqrt(ssq * (1.0 / ATT_HD) + EPS) * qn_ref[...]
    q = _rope(q, jnp.concatenate([cos] * 4, axis=1), jnp.concatenate([sin] * 4, axis=1), ATT_NH * ATT_HD)
    q = q * QK_SCALE
    for g in range(ATT_NKV):
        qg = q[:, g * 256:(g + 1) * 256].T
        hi = qg.astype(F8)
        lo = (qg - hi.astype(F32)).astype(F8)
        for h in range(ATT_GRP):
            for part, src in enumerate((hi, hi, lo, lo)):
                r0 = h * 4 * ATT_HD + part * ATT_HD
                qt_ref[0, g, r0:r0 + ATT_HD, :] = src[h * ATT_HD:(h + 1) * ATT_HD]
    k = k_ref[0].astype(F32)
    ssk = jnp.dot((k * k).astype(BF16), ones_ref[0:128, 0:128], preferred_element_type=F32)
    k = k * lax.rsqrt(ssk * (1.0 / ATT_HD) + EPS) * kn_ref[...]
    k = _rope(k, cos, sin, ATT_NKV * ATT_HD) * QK_SCALE
    vt = v_ref[0].astype(F32).T
    pad_row = lax.broadcasted_iota(jnp.int32, (VT_ROWS - ATT_HD, tp), 0)
    pad = jnp.where(pad_row == 0, 1.0, 0.0).astype(vt_ref.dtype)
    for g in range(ATT_NKV):
        kh = k[:, g * ATT_HD:(g + 1) * ATT_HD]
        hi = kh.astype(F8)
        lo = (kh - hi.astype(F32)).astype(F8)
        for part, src in enumerate((hi, lo, hi, lo)):
            ko_ref[0, g, :, part * ATT_HD:(part + 1) * ATT_HD] = src
        vt_ref[0, g, 0:ATT_HD, :] = vt[g * ATT_HD:(g + 1) * ATT_HD].astype(vt_ref.dtype)
        vt_ref[0, g, ATT_HD:VT_ROWS, :] = pad


def _qkv_prep(z_all, cos, sin, q_norm, k_norm, tp):
    b, nt, _ = z_all.shape
    head = jnp.arange(ATT_NH * ATT_HD) // ATT_HD
    ones = (head[:, None] == head[None, :]).astype(BF16)
    return pl.pallas_call(
        _qkv_kernel,
        grid=(b, nt // tp),
        in_specs=[pl.BlockSpec((1, tp, 512), lambda bi, i: (bi, i, C_Q // 512)),
                  pl.BlockSpec((1, tp, 128), lambda bi, i: (bi, i, C_K // 128)),
                  pl.BlockSpec((1, tp, 128), lambda bi, i: (bi, i, C_V // 128)),
                  pl.BlockSpec((tp, 128), lambda bi, i: (i, 0)),
                  pl.BlockSpec((tp, 128), lambda bi, i: (i, 0)),
                  pl.BlockSpec((1, 512), lambda bi, i: (0, 0)),
                  pl.BlockSpec((1, 128), lambda bi, i: (0, 0)),
                  pl.BlockSpec((512, 512), lambda bi, i: (0, 0))],
        out_specs=[pl.BlockSpec((1, ATT_NKV, 4 * 256, tp), lambda bi, i: (bi, 0, 0, i)),
                   pl.BlockSpec((1, ATT_NKV, tp, 4 * ATT_HD), lambda bi, i: (bi, 0, i, 0)),
                   pl.BlockSpec((1, ATT_NKV, VT_ROWS, tp), lambda bi, i: (bi, 0, 0, i))],
        out_shape=[jax.ShapeDtypeStruct((b, ATT_NKV, 4 * 256, nt), F8),
                   jax.ShapeDtypeStruct((b, ATT_NKV, nt, 4 * ATT_HD), F8),
                   jax.ShapeDtypeStruct((b, ATT_NKV, VT_ROWS, nt), BF16)],
        compiler_params=_cparams(("parallel", "parallel")),
        name="qkv_prep",
    )(z_all, z_all, z_all, cos, sin, jnp.tile(q_norm.astype(F32), ATT_NH).reshape(1, 512),
      jnp.tile(k_norm.astype(F32), ATT_NKV).reshape(1, 128), ones)


def _rope_tables(n_lat, nt):
    half = ATT_HD // 4
    inv_freq = ROPE_THETA ** (-jnp.arange(half, dtype=F32) / half)
    pos = jnp.arange(n_lat)
    row = (pos // GRID_W).astype(F32)
    col = (pos % GRID_W).astype(F32)
    ang = jnp.concatenate([row[:, None] * inv_freq, col[:, None] * inv_freq], axis=-1)
    cos = jnp.concatenate([jnp.cos(ang), jnp.ones((nt - n_lat, 2 * half), F32)], axis=0)
    sin = jnp.concatenate([jnp.sin(ang), jnp.zeros((nt - n_lat, 2 * half), F32)], axis=0)
    return jnp.tile(cos, (1, 4)), jnp.tile(sin, (1, 4))


def _attn_kernel(qt_ref, k_ref, vt_ref, o_ref, s0_ref, s1_ref, m_ref, acc_ref, *, nk, tk):
    sbuf = (s0_ref, s1_ref)
    m_ref[...] = jnp.full_like(m_ref, -jnp.inf)
    acc_ref[...] = jnp.zeros_like(acc_ref)

    def score(j, h):
        k = k_ref[0, 0, pl.ds(pl.multiple_of(j * tk, tk), tk), :]
        s = jnp.dot(k, qt_ref[0, 0, h * 4 * ATT_HD:(h + 1) * 4 * ATT_HD, :], preferred_element_type=F32)
        sbuf[h % 2][...] = s
        return jnp.max(s, axis=0, keepdims=True)

    def fold(j, h, tile_max):
        vt = vt_ref[0, 0, :, pl.ds(pl.multiple_of(j * tk, tk), tk)]
        m_old = m_ref[h]
        m_new = jnp.maximum(m_old, tile_max)
        p = jnp.exp2(sbuf[h % 2][...] - m_new).astype(BF16)
        acc_ref[h] = jnp.exp2(m_old - m_new) * acc_ref[h] + jnp.dot(vt, p, preferred_element_type=F32)
        m_ref[h] = m_new

    def key_tile(j, prev_max, first):
        tile_max = score(j, 0)
        if not first:
            fold(j - 1, ATT_GRP - 1, prev_max)
        for h in range(1, ATT_GRP):
            next_max = score(j, h)
            fold(j, h - 1, tile_max)
            tile_max = next_max
        return tile_max

    last_max = key_tile(0, None, True)
    last_max = lax.fori_loop(1, nk, lambda j, pm: key_tile(j, pm, False), last_max)
    fold(nk - 1, ATT_GRP - 1, last_max)

    outs = []
    for h in range(ATT_GRP):
        acc = acc_ref[h]
        outs.append(acc[0:ATT_HD] / acc[ATT_HD:ATT_HD + 1])
    o_ref[0] = jnp.concatenate(outs, axis=0).T.astype(o_ref.dtype)


def _attention(qt, k, vt, nq, tq, tk):
    b = qt.shape[0]
    nkeys = k.shape[2]
    return pl.pallas_call(
        functools.partial(_attn_kernel, nk=nkeys // tk, tk=tk),
        grid=(b, ATT_NKV, nq // tq),
        in_specs=[pl.BlockSpec((1, 1, 4 * 256, tq), lambda bi, g, i: (bi, g, 0, i)),
                  pl.BlockSpec((1, 1, nkeys, 4 * ATT_HD), lambda bi, g, i: (bi, g, 0, 0)),
                  pl.BlockSpec((1, 1, VT_ROWS, nkeys), lambda bi, g, i: (bi, g, 0, 0))],
        out_specs=pl.BlockSpec((1, tq, 256), lambda bi, g, i: (bi, i, g)),
        out_shape=jax.ShapeDtypeStruct((b, nq, ATT_NH * ATT_HD), BF16),
        scratch_shapes=[pltpu.VMEM((tk, tq), F32), pltpu.VMEM((tk, tq), F32),
                        pltpu.VMEM((ATT_GRP, 1, tq), F32), pltpu.VMEM((ATT_GRP, VT_ROWS, tq), F32)],
        compiler_params=_cparams(("parallel", "parallel", "parallel")),
        name="attention",
    )(qt, k, vt)


def _merge_kernel(x_ref, mg_ref, u_ref, ug_ref, z_ref, ag_ref, y5_ref, ysf_ref, ysr_ref, yc_ref,
                  mod_ref, s5d_ref, nrm_ref, pn_ref, glu_ref, wa_ref, wb_ref, wc_ref, wo_ref, o_ref, *, n_lat, tm):
    i = pl.program_id(1)
    f = lambda r: r[0].astype(F32)
    u = f(u_ref)
    ya = jax.nn.gelu(f(y5_ref) + s5d_ref[...] * u, approximate=True)
    ya = ya * jax.nn.sigmoid(jnp.dot(ya.astype(BF16), glu_ref[...], preferred_element_type=F32))
    ya = ya * _silu(f(ug_ref))
    yb = (f(ysf_ref) + f(ysr_ref)) * _silu(f(z_ref))
    yb = yb * lax.rsqrt(jnp.mean(yb * yb, axis=-1, keepdims=True) + EPS) * nrm_ref[...]
    yc = f(yc_ref) * _silu(f(ag_ref))
    gates = jax.nn.sigmoid(f(mg_ref))
    mix = gates[:, 0:D] * jnp.dot(ya.astype(BF16), wa_ref[...], preferred_element_type=F32)
    mix += gates[:, D:2 * D] * jnp.dot(yb.astype(BF16), wb_ref[...], preferred_element_type=F32)
    mix += gates[:, 2 * D:3 * D] * jnp.dot(yc.astype(BF16), wc_ref[...], preferred_element_type=F32)
    out = jnp.dot(mix.astype(BF16), wo_ref[...], preferred_element_type=F32)
    out = out * lax.rsqrt(jnp.mean(out * out, axis=-1, keepdims=True) + EPS) * pn_ref[...]
    m = mod_ref[0]
    row = i * tm + lax.broadcasted_iota(jnp.int32, (tm, 1), 0)
    gate = jnp.where(row >= n_lat, m[5:6], m[2:3])
    o_ref[0] = x_ref[0] + gate * out


def _merge(x_all, z_all, y5, ysf, ysr, yc, modv, s5_d, ssd_norm, post_norm, glu, wa, wb, wc, wo,
           n_lat, n_rows, tm):
    b = x_all.shape[0]
    tok = lambda w, c: pl.BlockSpec((1, tm, w), lambda bi, i: (bi, i, c))
    full = lambda shape: pl.BlockSpec(shape, lambda bi, i: (0,) * len(shape))
    return pl.pallas_call(
        functools.partial(_merge_kernel, n_lat=n_lat, tm=tm),
        grid=(b, n_rows // tm),
        in_specs=[tok(D, 0), tok(3 * D, 0), tok(512, C_U // 512), tok(512, C_UG // 512), tok(512, C_Z // 512),
                  tok(512, C_AG // 512), tok(512, 0), tok(512, 0), tok(512, 0), tok(512, 0),
                  pl.BlockSpec((1, 8, D), lambda bi, i: (bi, 0, 0)),
                  full((1, 512)), full((1, 512)), full((1, D)),
                  full((512, 512)), full((512, D)), full((512, D)), full((512, D)), full((D, D))],
        out_specs=tok(D, 0),
        out_shape=jax.ShapeDtypeStruct((b, n_rows, D), F32),
        compiler_params=_cparams(("parallel", "parallel")),
        name="merge",
    )(x_all, z_all, z_all, z_all, z_all, z_all, y5, ysf, ysr, yc, modv,
      s5_d.astype(F32).reshape(1, 512), ssd_norm.astype(F32).reshape(1, 512), post_norm.astype(F32).reshape(1, D),
      glu.astype(BF16), wa.astype(BF16), wb.astype(BF16), wc.astype(BF16), wo.astype(BF16))


def _largest_divisor(n, cands):
    for c in cands:
        if n % c == 0:
            return c
    raise ValueError(f"no tile in {cands} divides {n}")


def _layer(x_all, mod_l, cos, sin, n_lat, with_ctx, pre_norm, post_norm, w_in, s5_lam_re, s5_lam_im, s5_log_dt,
           s5_b_re, s5_b_im, s5_c_re, s5_c_im, s5_d, s5_glu, ssd_conv_w, ssd_conv_b, ssd_dt_bias, ssd_a_log, ssd_d,
           ssd_norm, att_q_norm, att_k_norm, w_branch_a, w_branch_b, w_branch_c, w_out):
    bsz, nt, _ = x_all.shape
    n_ctx = nt - n_lat
    tm_proj = CHUNK * _largest_divisor(nt // CHUNK, (5, 4, 3, 2, 1))
    tq = _largest_divisor(n_lat, (512, 256))
    tk = CHUNK * _largest_divisor(nt // CHUNK, (5, 4, 3, 2, 1))
    ml = mod_l[:bsz].reshape(bsz, 3, D)
    mc = jnp.broadcast_to(mod_l[bsz].reshape(1, 3, D), (bsz, 3, D))
    modv = jnp.concatenate([ml, mc, jnp.zeros((bsz, 2, D), F32)], axis=1)
    z_all = _in_proj(x_all, modv, pre_norm.astype(F32), _pack_w_in(w_in), n_lat, tm_proj)

    tabs = _s5_tables(s5_lam_re, s5_lam_im, s5_log_dt, s5_b_re, s5_b_im, s5_c_re, s5_c_im)
    y5 = _s5_scan(z_all, tabs, tm_proj)

    dtt = z_all[:, :, C_DT:C_DT + 2 * SSD_NH].astype(F32).transpose(0, 2, 1)
    ssd_args = (z_all, dtt, ssd_conv_w, ssd_conv_b, ssd_dt_bias, ssd_a_log, ssd_d)
    ysf = _ssd_direction(*ssd_args, rev=False)
    ysr = _ssd_direction(*ssd_args, rev=True)

    qt, kk, vt = _qkv_prep(z_all, cos, sin, att_q_norm, att_k_norm, CHUNK)
    yc = _attention(qt, kk, vt, n_lat, tq, tk)
    if with_ctx:
        yc_ctx = _attention(qt[..., n_lat:], kk[:, :, n_lat:], vt[..., n_lat:], n_ctx, n_ctx, n_ctx)
        yc = jnp.concatenate([yc, yc_ctx], axis=1)
    n_rows = nt if with_ctx else n_lat
    return _merge(x_all, z_all, y5, ysf, ysr, yc, modv, s5_d, ssd_norm, post_norm,
                  s5_glu, w_branch_a, w_branch_b, w_branch_c, w_out, n_lat, n_rows, CHUNK)


def kernel(x, c, ctx, c_ctx, mod_w, mod_b, pre_norm, post_norm, w_in, s5_lam_re, s5_lam_im, s5_log_dt, s5_b_re, s5_b_im, s5_c_re, s5_c_im, s5_d, s5_glu, ssd_conv_w, ssd_conv_b, ssd_dt_bias, ssd_a_log, ssd_d, ssd_norm, att_q_norm, att_k_norm, w_branch_a, w_branch_b, w_branch_c, w_out):
    bsz, n_lat, _ = x.shape
    n_ctx = ctx.shape[1]
    depth = mod_w.shape[0]
    nt = n_lat + n_ctx
    assert n_lat % CHUNK == 0 and n_ctx == CHUNK and n_lat % GRID_W == 0 and bsz + 1 <= 8
    c8 = jnp.concatenate([c.astype(F32), c_ctx.astype(F32)[None], jnp.zeros((8 - bsz - 1, D), F32)], axis=0)
    mod = _modulation(c8, mod_w.astype(F32), mod_b.astype(F32))
    cos, sin = _rope_tables(n_lat, nt)
    x_all = jnp.concatenate([x.astype(F32), ctx.astype(F32)], axis=1)
    per_layer = (pre_norm, post_norm, w_in, s5_lam_re, s5_lam_im, s5_log_dt, s5_b_re, s5_b_im, s5_c_re, s5_c_im,
                 s5_d, s5_glu, ssd_conv_w, ssd_conv_b, ssd_dt_bias, ssd_a_log, ssd_d, ssd_norm, att_q_norm,
                 att_k_norm, w_branch_a, w_branch_b, w_branch_c, w_out)
    for l in range(depth):
        x_all = _layer(x_all, mod[l], cos, sin, n_lat, l < depth - 1, *[p[l] for p in per_layer])
    return x_all.astype(x.dtype)
```
